```python
import functools
import jax, jax.numpy as jnp
from jax import lax
import numpy as np

D_MODEL = 1024
BATCH = 4
SEQ = 4096
DEPTH = 4
DEC_BATCH = 32
DEC_SEQ = 4
PAST_LEN = 8192
PAGE_SIZE = 128

H_A = 8
HD_A = 64
W_A = H_A * HD_A
R_DECAY = 64
R_AAA = 64
R_GATE = 128
H_B = 8
HD_B = 64
W_B = H_B * HD_B
BLOCK_Q = 128
H_C = 4
HD_C = 128
W_C = H_C * HD_C
N_MEM = 256
N_BRANCH = 3
N_SHIFT = 3 * W_A + R_DECAY + R_AAA + R_GATE
N_IN = N_SHIFT + 3 * W_B + W_C + N_BRANCH * D_MODEL
IN_SPLITS = [N_SHIFT, N_SHIFT + 3 * W_B, N_SHIFT + 3 * W_B + W_C]
SHIFT_SPLITS = [W_A, 2 * W_A, 3 * W_A, 3 * W_A + R_DECAY, 3 * W_A + R_DECAY + R_AAA]
N_EXPERTS = 16
N_GROUPS = 4
E_PER_GROUP = N_EXPERTS // N_GROUPS
TOP_K = 2
D_FF = 512
DN_ALPHA = (2 * DEPTH) ** 0.25
DN_BETA = (8 * DEPTH) ** -0.25
LN_EPS = 1e-5
GN_EPS = 64e-5
NORM_EPS = 1e-12

kernel_name = 'rwkv7_stickbreak_memxattn_sharedrouter_moe_step'


def layer_norm(x, g, b):
    xf = x.astype(jnp.float32)
    mu = jnp.mean(xf, axis=-1, keepdims=True)
    var = jnp.mean(jnp.square(xf - mu), axis=-1, keepdims=True)
    return ((xf - mu) * lax.rsqrt(var + LN_EPS)).astype(x.dtype) * g + b


def split_heads(t, n_heads, head_dim):
    return t.reshape(*t.shape[:-1], n_heads, head_dim)


def wkv7_scan(state0, r, decay, k, v, kk, a):
    def step(S, inp):
        r_t, w_t, k_t, v_t, kk_t, a_t = inp
        s_kk = jnp.einsum('bhvk,bhk->bhv', S, kk_t)
        S = (S * w_t[:, :, None, :] - s_kk[..., None] * (kk_t * a_t)[:, :, None, :]
             + v_t[..., None] * k_t[:, :, None, :])
        return S, jnp.einsum('bhvk,bhk->bhv', S, r_t)
    xs = tuple(jnp.swapaxes(t.astype(jnp.float32), 0, 1) for t in (r, decay, k, v, kk, a))
    S, o = lax.scan(step, state0.astype(jnp.float32), xs)
    return S, jnp.swapaxes(o, 0, 1)


def rwkv7_branch(p, shift0, wkv0, lp):
    dt = p.dtype
    prev = jnp.concatenate([shift0[:, None, :].astype(dt), p[:, :-1]], axis=1)
    z = p + lp['mu_shift'] * (prev - p)
    r, k, v, wl, al, gl = jnp.split(z, SHIFT_SPLITS, axis=-1)
    w_log = -jax.nn.softplus(-(lp['w0_decay'] + jnp.tanh(wl) @ lp['w_decay2']).astype(jnp.float32)) - 0.5
    decay = jnp.exp(-jnp.exp(w_log))
    a = jax.nn.sigmoid(lp['a0'] + al @ lp['w_a2'])
    g = jax.nn.sigmoid(gl) @ lp['w_g2']
    kk = split_heads(k * lp['k_k'], H_A, HD_A).astype(jnp.float32)
    kk = kk / jnp.maximum(jnp.sqrt(jnp.sum(jnp.square(kk), axis=-1, keepdims=True)), NORM_EPS)
    k = k * (1.0 + (a - 1.0) * lp['k_a'])
    rh, kh, vh, ah = (split_heads(t, H_A, HD_A) for t in (r, k, v, a))
    wkv, o = wkv7_scan(wkv0, rh, split_heads(decay, H_A, HD_A), kh, vh, kk, ah)
    mu = jnp.mean(o, axis=-1, keepdims=True)
    var = jnp.mean(jnp.square(o - mu), axis=-1, keepdims=True)
    o = ((o - mu) * lax.rsqrt(var + GN_EPS)).reshape(*p.shape[:-1], W_A).astype(dt) * lp['gn_g'] + lp['gn_b']
    bonus = jnp.sum(rh * kh * lp['r_k'], axis=-1, keepdims=True) * vh
    o = (o + bonus.reshape(o.shape)) * g
    return o, wkv, p[:, -1]


def stick_breaking(q, k, v, bias, q_pos, k_pos):
    z = (jnp.einsum('bqhd,bkhd->bhqk', q, k).astype(jnp.float32) * (HD_B ** -0.5)
         + bias.astype(jnp.float32)[None, :, None, None])
    mask = k_pos[None, :] < q_pos[:, None]
    log_not = jnp.where(mask, jax.nn.log_sigmoid(-z), 0.0)
    tail = lax.cumsum(log_not, axis=3, reverse=True) - log_not
    att = jnp.where(mask, jnp.exp(jax.nn.log_sigmoid(z) + tail), 0.0)
    return jnp.einsum('bhqk,bkhd->bqhd', att.astype(v.dtype), v)


def sb_prompt(q, k, v, bias):
    B, T = q.shape[:2]
    nb = T // BLOCK_Q
    pos = jnp.arange(T, dtype=jnp.int32)
    qb = jnp.swapaxes(q.reshape(B, nb, BLOCK_Q, H_B, HD_B), 0, 1)
    out = lax.map(lambda blk: stick_breaking(blk[0], k, v, bias, blk[1], pos),
                  (qb, pos.reshape(nb, BLOCK_Q)))
    return jnp.swapaxes(out, 0, 1).reshape(B, T, W_B)


def sb_sample(q, k, v, bias, k_cache, v_cache, page_table):
    B, T = q.shape[:2]
    k_past = k_cache[page_table].reshape(B, -1, H_B, HD_B)
    v_past = v_cache[page_table].reshape(B, -1, H_B, HD_B)
    past = k_past.shape[1]
    k_all = jnp.concatenate([k_past, k.astype(k_past.dtype)], axis=1)
    v_all = jnp.concatenate([v_past, v.astype(v_past.dtype)], axis=1)
    k_pos = jnp.arange(past + T, dtype=jnp.int32)
    q_pos = past + jnp.arange(T, dtype=jnp.int32)
    return stick_breaking(q, k_all, v_all, bias, q_pos, k_pos).reshape(B, T, W_B)


def mem_attend(q, mem_k, mem_v):
    s = jnp.einsum('bqhd,bmhd->bhqm', q, mem_k).astype(jnp.float32) * (HD_C ** -0.5)
    prob = jax.nn.softmax(s, axis=-1)
    o = jnp.einsum('bhqm,bmhd->bqhd', prob.astype(mem_v.dtype), mem_v)
    return o.reshape(*q.shape[:-2], W_C)


def moe_ffn(x, w_router, b_router, w1, w3, w2):
    logits = (x @ w_router + b_router).astype(jnp.float32)
    scores = jax.nn.softmax(logits, axis=-1)
    grouped = scores.reshape(*scores.shape[:-1], N_GROUPS, E_PER_GROUP)
    group_score = jnp.sum(lax.top_k(grouped, TOP_K)[0], axis=-1)
    g_sel = jnp.argmax(group_score, axis=-1)
    in_group = (jnp.arange(N_EXPERTS) // E_PER_GROUP) == g_sel[..., None]
    top_w, top_i = lax.top_k(jnp.where(in_group, scores, -1.0), TOP_K)
    top_w = top_w / jnp.sum(top_w, axis=-1, keepdims=True)
    gate = jnp.sum(jax.nn.one_hot(top_i, N_EXPERTS, dtype=jnp.float32) * top_w[..., None], axis=-2)
    h = jax.nn.silu(jnp.einsum('btd,edf->btef', x, w1)) * jnp.einsum('btd,edf->btef', x, w3)
    return jnp.einsum('btef,efd->btd', h * gate.astype(h.dtype)[..., None], w2)


def trunk_layer(x, lp, sb_fn, mem_k, mem_v, wkv0, shift0, w_router, b_router):
    p = x @ lp['w_in']
    p_shift, p_sb, p_cq, p_gate = jnp.split(p, IN_SPLITS, axis=-1)
    o_a, wkv, shift = rwkv7_branch(p_shift, shift0, wkv0, lp)
    q_b, k_b, v_b = (split_heads(t, H_B, HD_B) for t in jnp.split(p_sb, 3, axis=-1))
    o_b = sb_fn(q_b, k_b, v_b, lp['sb_bias'])
    o_c = mem_attend(split_heads(p_cq, H_C, HD_C), mem_k, mem_v)
    gate = jax.nn.sigmoid(p_gate).reshape(*x.shape[:-1], N_BRANCH, D_MODEL)
    h = (gate[..., 0, :] * (o_a @ lp['w_br_a']) + gate[..., 1, :] * (o_b @ lp['w_br_b'])
         + gate[..., 2, :] * (o_c @ lp['w_br_c']))
    x = layer_norm(DN_ALPHA * x + h @ lp['w_out'], lp['ln1_g'], lp['ln1_b'])
    y = moe_ffn(x, w_router, b_router, lp['w1'], lp['w3'], lp['w2'])
    x = layer_norm(DN_ALPHA * x + y, lp['ln2_g'], lp['ln2_b'])
    return x, k_b, v_b, wkv, shift


def setup_inputs(seed: int = 0) -> dict:
    key = jax.random.key(seed)
    ks = jax.random.split(key, 40)
    f32 = jnp.float32

    def nrm(i, shape, scale=1.0):
        return jax.random.normal(ks[i], shape, f32) * scale

    L = DEPTH
    n_pages = PAST_LEN // PAGE_SIZE
    n_used = DEC_BATCH * n_pages
    n_pool = n_used + max(1, n_used // 4)
    page_table = jax.random.permutation(ks[0], n_pool)[:n_used].reshape(DEC_BATCH, n_pages).astype(jnp.int32)
    return {
        'x_prompt': nrm(1, (BATCH, SEQ, D_MODEL)),
        'x_sample': nrm(2, (DEC_BATCH, DEC_SEQ, D_MODEL)),
        'mem_prompt': nrm(3, (BATCH, N_MEM, D_MODEL)),
        'cache_sb_k': nrm(4, (L, n_pool, PAGE_SIZE, H_B, HD_B)),
        'cache_sb_v': nrm(5, (L, n_pool, PAGE_SIZE, H_B, HD_B)),
        'cache_mem_k': nrm(6, (L, DEC_BATCH, N_MEM, H_C, HD_C)),
        'cache_mem_v': nrm(7, (L, DEC_BATCH, N_MEM, H_C, HD_C)),
        'state_wkv': nrm(8, (L, DEC_BATCH, H_A, HD_A, HD_A), 0.5),
        'state_shift': nrm(9, (L, DEC_BATCH, N_SHIFT)),
        'page_table': page_table,
        'w_in': nrm(10, (L, D_MODEL, N_IN), D_MODEL ** -0.5),
        'mu_shift': jax.random.uniform(ks[11], (L, N_SHIFT), f32),
        'w0_decay': jax.random.uniform(ks[12], (L, W_A), f32, -6.0, -1.0),
        'w_decay2': nrm(13, (L, R_DECAY, W_A), 0.1 * R_DECAY ** -0.5),
        'a0': nrm(14, (L, W_A), 0.1),
        'w_a2': nrm(15, (L, R_AAA, W_A), 0.1 * R_AAA ** -0.5),
        'w_g2': nrm(16, (L, R_GATE, W_A), R_GATE ** -0.5),
        'k_k': 0.85 + nrm(17, (L, W_A), 0.02),
        'k_a': 1.0 + nrm(18, (L, W_A), 0.02),
        'r_k': nrm(19, (L, H_A, HD_A), 0.1),
        'gn_g': 1.0 + nrm(20, (L, W_A), 0.02),
        'gn_b': nrm(21, (L, W_A), 0.02),
        'sb_bias': jax.random.uniform(ks[36], (L, H_B), f32, -10.0, -6.0),
        'w_mem_kv': nrm(22, (L, D_MODEL, 2 * W_C), D_MODEL ** -0.5),
        'w_br_a': nrm(23, (L, W_A, D_MODEL), W_A ** -0.5),
        'w_br_b': nrm(24, (L, W_B, D_MODEL), W_B ** -0.5),
        'w_br_c': nrm(25, (L, W_C, D_MODEL), W_C ** -0.5),
        'w_out': nrm(26, (L, D_MODEL, D_MODEL), DN_BETA * D_MODEL ** -0.5),
        'ln1_g': 1.0 + nrm(27, (L, D_MODEL), 0.02),
        'ln1_b': nrm(28, (L, D_MODEL), 0.02),
        'w_router': nrm(29, (D_MODEL, N_EXPERTS), D_MODEL ** -0.5),
        'b_router': nrm(30, (N_EXPERTS,), 0.01),
        'w1': nrm(31, (L, N_EXPERTS, D_MODEL, D_FF), D_MODEL ** -0.5),
        'w3': nrm(32, (L, N_EXPERTS, D_MODEL, D_FF), D_MODEL ** -0.5),
        'w2': nrm(33, (L, N_EXPERTS, D_FF, D_MODEL), DN_BETA * D_FF ** -0.5),
        'ln2_g': 1.0 + nrm(34, (L, D_MODEL), 0.02),
        'ln2_b': nrm(35, (L, D_MODEL), 0.02),
    }


def reference(x_prompt, x_sample, mem_prompt, cache_sb_k, cache_sb_v, cache_mem_k, cache_mem_v,
              state_wkv, state_shift, page_table, w_in, mu_shift, w0_decay, w_decay2, a0, w_a2,
              w_g2, k_k, k_a, r_k, gn_g, gn_b, sb_bias, w_mem_kv, w_br_a, w_br_b, w_br_c, w_out,
              ln1_g, ln1_b, w_router, b_router, w1, w3, w2, ln2_g, ln2_b):
    zero_wkv = jnp.zeros((x_prompt.shape[0], H_A, HD_A, HD_A), jnp.float32)
    zero_shift = jnp.zeros((x_prompt.shape[0], N_SHIFT), x_prompt.dtype)
    xp, xs = x_prompt, x_sample
    sbkp, sbvp, sbks, sbvs = [], [], [], []
    mkp, mvp, wkp, wks, shp, shs = [], [], [], [], [], []
    for l in range(DEPTH):
        lp = {'w_in': w_in[l], 'mu_shift': mu_shift[l], 'w0_decay': w0_decay[l],
              'w_decay2': w_decay2[l], 'a0': a0[l], 'w_a2': w_a2[l], 'w_g2': w_g2[l],
              'k_k': k_k[l], 'k_a': k_a[l], 'r_k': r_k[l], 'gn_g': gn_g[l], 'gn_b': gn_b[l],
              'sb_bias': sb_bias[l],
              'w_br_a': w_br_a[l], 'w_br_b': w_br_b[l], 'w_br_c': w_br_c[l], 'w_out': w_out[l],
              'ln1_g': ln1_g[l], 'ln1_b': ln1_b[l], 'w1': w1[l], 'w3': w3[l], 'w2': w2[l],
              'ln2_g': ln2_g[l], 'ln2_b': ln2_b[l]}
        mem_kv = mem_prompt @ w_mem_kv[l]
        mk, mv = (split_heads(t, H_C, HD_C) for t in jnp.split(mem_kv, 2, axis=-1))
        xp, kb, vb, wkv, sh = trunk_layer(xp, lp, sb_prompt, mk, mv, zero_wkv, zero_shift,
                                          w_router, b_router)
        sbkp.append(kb); sbvp.append(vb); mkp.append(mk); mvp.append(mv)
        wkp.append(wkv); shp.append(sh)
        sb_fn = functools.partial(sb_sample, k_cache=cache_sb_k[l], v_cache=cache_sb_v[l],
                                  page_table=page_table)
        xs, kb, vb, wkv, sh = trunk_layer(xs, lp, sb_fn, cache_mem_k[l], cache_mem_v[l],
                                          state_wkv[l], state_shift[l], w_router, b_router)
        sbks.append(kb); sbvs.append(vb); wks.append(wkv); shs.append(sh)
    return (xp, xs, jnp.stack(sbkp), jnp.stack(sbvp), jnp.stack(sbks), jnp.stack(sbvs),
            jnp.stack(mkp), jnp.stack(mvp), jnp.stack(wkp), jnp.stack(wks),
            jnp.stack(shp), jnp.stack(shs))
```

```python
import functools

import numpy as np
import jax
import jax.numpy as jnp
from jax import lax
from jax.experimental import pallas as pl
from jax.experimental.pallas import tpu as pltpu

F32 = jnp.float32
BF16 = jnp.bfloat16

D_MODEL = 1024
DEPTH = 4
N_HEAD = 8
HEAD = 64
W_A = N_HEAD * HEAD
R_DECAY, R_AAA, R_GATE = 64, 64, 128
H_C, HD_C = 4, 128
W_C = H_C * HD_C
N_MEM = 256
N_SHIFT = 3 * W_A + R_DECAY + R_AAA + R_GATE
N_GATE = 3 * D_MODEL
N_IN = N_SHIFT + 3 * W_A + W_C + N_GATE
N_EXPERTS, N_GROUPS, E_PER_GROUP, D_FF = 16, 4, 4, 512
DN_ALPHA = (2 * DEPTH) ** 0.25
LN_EPS = 1e-5
GN_EPS = 64e-5
NORM_EPS = 1e-12
PAGE = 128
LANE = 128
SUBLANE = 8
T_PAD = SUBLANE
VMEM_LIMIT = 56 * 1024 * 1024


def _cparams(sem):
    return pltpu.CompilerParams(dimension_semantics=sem, vmem_limit_bytes=VMEM_LIMIT)


def _softplus(z):
    return jnp.maximum(z, 0.0) + jnp.log1p(jnp.exp(-jnp.abs(z)))


def _sigmoid(z):
    return 1.0 / (1.0 + jnp.exp(-z))


def _dot(a, b):
    return jnp.dot(a, b, preferred_element_type=F32)


def _dot_nt(a, b):
    return lax.dot_general(a, b, (((1,), (1,)), ((), ())), preferred_element_type=F32)


def _bf16_pieces(a, n):
    pieces = []
    for _ in range(n):
        p = a.astype(BF16)
        pieces.append(p)
        a = a - p.astype(F32)
    return pieces


def _dot_f32(a, b, ca=1, cb=0):
    ah, al = _bf16_pieces(a, 2)
    bh, bl = _bf16_pieces(b, 2)
    return lax.dot_general(jnp.concatenate([ah, ah, al], axis=ca), jnp.concatenate([bh, bl, bh], axis=cb),
                           (((ca,), (cb,)), ((), ())), preferred_element_type=F32)


def _dot_mask_f32(mask, b):
    m = mask.astype(BF16)
    return _dot(jnp.concatenate([m, m, m], axis=1), jnp.concatenate(_bf16_pieces(b, 3), axis=0))


def _layer_norm(y, g, b):
    mu = jnp.mean(y, axis=-1, keepdims=True)
    yc = y - mu
    var = jnp.mean(yc * yc, axis=-1, keepdims=True)
    return yc * lax.rsqrt(var + LN_EPS) * g + b


_IN_GROUPS = (N_SHIFT, W_A, W_A, W_A, W_C, N_GATE)
_MM_COLS = 256


def _in_proj_body(x_ref, w_ref, *out_refs):
    x = x_ref[...].astype(BF16)
    off = 0
    for ref, width in zip(out_refs, _IN_GROUPS):
        for c in range(0, width, _MM_COLS):
            ref[:, c:c + _MM_COLS] = _dot(x, w_ref[:, off + c:off + c + _MM_COLS])
        off += width


def _in_proj(x, w_in_bf16):
    n = x.shape[0]
    bm = min(n, 256)
    return pl.pallas_call(
        _in_proj_body,
        grid=(n // bm,),
        in_specs=[pl.BlockSpec((bm, D_MODEL), lambda i: (i, 0)),
                  pl.BlockSpec((D_MODEL, N_IN), lambda i: (0, 0), pipeline_mode=pl.Buffered(1))],
        out_specs=[pl.BlockSpec((bm, w), lambda i: (i, 0)) for w in _IN_GROUPS],
        out_shape=[jax.ShapeDtypeStruct((n, w), F32) for w in _IN_GROUPS],
        compiler_params=_cparams(("parallel",)),
        name="in_proj",
    )(x, w_in_bf16)


def _matmul_body(x_ref, w_ref, o_ref):
    o_ref[...] = _dot(x_ref[...].astype(BF16), w_ref[...])


def _matmul(x, w_bf16, bn=512):
    n, k = x.shape
    m = w_bf16.shape[1]
    bm = min(n, 256)
    return pl.pallas_call(
        _matmul_body,
        grid=(n // bm, m // bn),
        in_specs=[pl.BlockSpec((bm, k), lambda i, j: (i, 0)), pl.BlockSpec((k, bn), lambda i, j: (0, j))],
        out_specs=pl.BlockSpec((bm, bn), lambda i, j: (i, j)),
        out_shape=jax.ShapeDtypeStruct((n, m), F32),
        compiler_params=_cparams(("parallel", "parallel")),
        name="matmul",
    )(x, w_bf16)


def _head_sum(x):
    n = x.shape[-1]
    same = (lax.broadcasted_iota(jnp.int32, (n, n), 0) // HEAD) == (lax.broadcasted_iota(jnp.int32, (n, n), 1) // HEAD)
    same = same.astype(BF16)
    return _dot(jnp.concatenate(_bf16_pieces(x, 3), axis=1), jnp.concatenate([same, same, same], axis=0))


def _rwkv_prep_body(p_ref, sh0_ref, mu_ref, w0_ref, wd2_ref, a0_ref, wa2_ref, wg2_ref, kk_ref, ka_ref, rk_ref,
                    r_o, lw_o, k_o, v_o, kn_o, bb_o, g_o, bonus_o, shift_o, carry, *, tb, t_valid, t_total):
    t = pl.program_id(1)

    @pl.when(t == 0)
    def _():
        carry[...] = sh0_ref[0]

    p = p_ref[...]
    row = lax.broadcasted_iota(jnp.int32, (tb, 1), 0)
    prev = jnp.where(row == 0, carry[...], pltpu.roll(p, 1, 0))
    carry[...] = p[tb - 1:tb, :]

    last = t_valid - 1
    @pl.when(t == last // tb)
    def _():
        shift_o[0] = p[last % tb:last % tb + 1, :]

    z = p + mu_ref[...] * (prev - p)
    r = z[:, 0:W_A]
    k = z[:, W_A:2 * W_A]
    v = z[:, 2 * W_A:3 * W_A]
    c0 = 3 * W_A
    wl = z[:, c0:c0 + R_DECAY]
    al = z[:, c0 + R_DECAY:c0 + R_DECAY + R_AAA]
    gl = z[:, c0 + R_DECAY + R_AAA:N_SHIFT]

    w_log = -_softplus(-(w0_ref[...] + _dot(jnp.tanh(wl).astype(BF16), wd2_ref[...]))) - 0.5
    lw = -jnp.exp(w_log)
    a = _sigmoid(a0_ref[...] + _dot(al.astype(BF16), wa2_ref[...]))
    g = _dot(_sigmoid(gl).astype(BF16), wg2_ref[...])
    kn = k * kk_ref[...]
    kn = kn / jnp.maximum(jnp.sqrt(_head_sum(kn * kn)), NORM_EPS)
    k2 = k * (1.0 + (a - 1.0) * ka_ref[...])
    bonus = _head_sum(r * k2 * rk_ref[...]) * v
    bb = kn * a
    if t_valid < t_total:
        ok = (t * tb + row) < t_valid
        lw = jnp.where(ok, lw, 0.0)
        kn = jnp.where(ok, kn, 0.0)
        bb = jnp.where(ok, bb, 0.0)
        k2 = jnp.where(ok, k2, 0.0)
        v = jnp.where(ok, v, 0.0)
    r_o[...] = r
    lw_o[...] = lw
    k_o[...] = k2
    v_o[...] = v
    kn_o[...] = kn
    bb_o[...] = bb
    g_o[...] = g
    bonus_o[...] = bonus


def _rwkv_prep(p_shift, shift0, lw, *, n_seq, t_total, t_valid):
    tb = min(t_total, 256)
    nt = t_total // tb
    n = n_seq * t_total
    row_spec = lambda w: pl.BlockSpec((tb, w), lambda b, t: (b * nt + t, 0))
    par = lambda a: pl.BlockSpec(a.shape, lambda b, t: (0,) * a.ndim)
    params = (lw['mu_shift'], lw['w0_decay'], lw['w_decay2'], lw['a0'], lw['w_a2'], lw['w_g2'],
              lw['k_k'], lw['k_a'], lw['r_k'])
    outs = pl.pallas_call(
        functools.partial(_rwkv_prep_body, tb=tb, t_valid=t_valid, t_total=t_total),
        grid=(n_seq, nt),
        in_specs=[row_spec(N_SHIFT), pl.BlockSpec((1, 1, N_SHIFT), lambda b, t: (b, 0, 0))] + [par(a) for a in params],
        out_specs=[row_spec(W_A)] * 8 + [pl.BlockSpec((1, 1, N_SHIFT), lambda b, t: (b, 0, 0))],
        out_shape=[jax.ShapeDtypeStruct((n, W_A), F32)] * 8 + [jax.ShapeDtypeStruct((n_seq, 1, N_SHIFT), F32)],
        scratch_shapes=[pltpu.VMEM((1, N_SHIFT), F32)],
        compiler_params=_cparams(("parallel", "arbitrary")),
        name="rwkv_prep",
    )(p_shift, shift0.reshape(n_seq, 1, N_SHIFT), *params)
    return outs[:8], outs[8].reshape(n_seq, N_SHIFT)


_INV_BLOCK = 16


def _nilpotent_inverse(n, steps):
    inv = -n
    sq = n
    for _ in range(steps - 1):
        sq = _dot_f32(sq, sq)
        inv = inv + sq + _dot_f32(inv, sq)
    return inv


def _unit_lower_inverse_minus_i(n, c):
    blk = min(_INV_BLOCK, c)
    if blk == c:
        return _nilpotent_inverse(n, int(np.log2(c)))
    ri = lax.broadcasted_iota(jnp.int32, (c, c), 0)
    ci = lax.broadcasted_iota(jnp.int32, (c, c), 1)
    on_diag = (ri // blk) == (ci // blk)
    eye = (ri == ci).astype(F32)
    d_inv = _nilpotent_inverse(jnp.where(on_diag, n, 0.0), int(np.log2(blk))) + eye
    m = _dot_f32(d_inv, jnp.where(on_diag, 0.0, n))
    m_inv = _nilpotent_inverse(m, int(np.log2(c // blk))) + eye
    return _dot_f32(m_inv, d_inv) - eye


def _wkv_body(r_ref, lw_ref, k_ref, v_ref, kn_ref, bb_ref, g_ref, bonus_ref, s0_ref, gng_ref, gnb_ref,
              o_ref, sT_ref, state, o_scr, *, c):
    ci = pl.program_id(1)

    @pl.when(ci == 0)
    def _():
        state[...] = s0_ref[0]

    ri = lax.broadcasted_iota(jnp.int32, (c, c), 0)
    cj = lax.broadcasted_iota(jnp.int32, (c, c), 1)
    incl = ri >= cj
    strict = ri > cj

    lw = lw_ref[...]
    cs = _dot_mask_f32(incl, lw)
    tot = cs[c - 1:c, :]
    e_in = jnp.exp(cs)
    e_out = jnp.exp(-cs)
    e_end = jnp.exp(tot - cs)
    k = k_ref[...]
    bb = bb_ref[...]
    r_t = r_ref[...] * e_in
    kap = kn_ref[...] * jnp.exp(cs - lw)
    k_t = k * e_out
    b_t = bb * e_out
    k_d = k * e_end
    b_d = bb * e_end
    e_tot = jnp.exp(tot)
    v = v_ref[...]

    for h in range(N_HEAD):
        sl = slice(h * HEAD, (h + 1) * HEAD)
        kap_h, r_h, k_h, b_h, v_h = kap[:, sl], r_t[:, sl], k_t[:, sl], b_t[:, sl], v[:, sl]
        a_kk = jnp.where(strict, _dot_f32(kap_h, k_h, 1, 1), 0.0)
        a_bk = jnp.where(strict, _dot_f32(kap_h, b_h, 1, 1), 0.0)
        a_kr = jnp.where(incl, _dot_f32(r_h, k_h, 1, 1), 0.0)
        a_br = jnp.where(incl, _dot_f32(r_h, b_h, 1, 1), 0.0)
        t_inv = _unit_lower_inverse_minus_i(a_bk, c)
        rhs0 = _dot_f32(a_kk, v_h)
        u0 = rhs0 + _dot_f32(t_inv, rhs0)
        w1 = kap_h + _dot_f32(t_inv, kap_h)
        s_h = state[h]
        u = _dot_f32(w1, s_h, 1, 1) + u0
        o = _dot_f32(r_h, s_h, 1, 1) + _dot_f32(a_kr, v_h) - _dot_f32(a_br, u)
        state[h] = s_h * e_tot[:, sl] + _dot_f32(v_h, k_d[:, sl], 0, 0) - _dot_f32(u, b_d[:, sl], 0, 0)
        mu = jnp.mean(o, axis=-1, keepdims=True)
        oc = o - mu
        var = jnp.mean(oc * oc, axis=-1, keepdims=True)
        o_scr[:, sl] = oc * lax.rsqrt(var + GN_EPS)

    o_ref[...] = (o_scr[...] * gng_ref[...] + gnb_ref[...] + bonus_ref[...]) * g_ref[...]

    @pl.when(ci == pl.num_programs(1) - 1)
    def _():
        sT_ref[0] = state[...]


def _wkv(prep, state0, gn_g, gn_b, *, n_seq, t_total):
    c = min(t_total, 64)
    nc = t_total // c
    n = n_seq * t_total
    row = pl.BlockSpec((c, W_A), lambda b, i: (b * nc + i, 0))
    st = pl.BlockSpec((1, N_HEAD, HEAD, HEAD), lambda b, i: (b, 0, 0, 0))
    par = pl.BlockSpec((1, W_A), lambda b, i: (0, 0))
    return pl.pallas_call(
        functools.partial(_wkv_body, c=c),
        grid=(n_seq, nc),
        in_specs=[row] * 8 + [st, par, par],
        out_specs=[row, st],
        out_shape=[jax.ShapeDtypeStruct((n, W_A), F32), jax.ShapeDtypeStruct((n_seq, N_HEAD, HEAD, HEAD), F32)],
        scratch_shapes=[pltpu.VMEM((N_HEAD, HEAD, HEAD), F32), pltpu.VMEM((c, W_A), F32)],
        compiler_params=_cparams(("parallel", "arbitrary")),
        name="wkv",
    )(*prep, state0, gn_g, gn_b)


def _later_key_mask():
    j = lax.broadcasted_iota(jnp.int32, (LANE, LANE), 0)
    s = lax.broadcasted_iota(jnp.int32, (LANE, LANE), 1)
    return (j > s).astype(BF16)


def _sb_tile(q, k, v, bias, carry, later, visible=None):
    z = _dot_nt(q, k) + bias
    sp = _softplus(z)
    ln = -sp
    if visible is not None:
        ln = jnp.where(visible, ln, 0.0)
    tail = _dot(ln.astype(BF16), later)
    att = jnp.exp(z - sp + tail + carry)
    if visible is not None:
        att = jnp.where(visible, att, 0.0)
    out = _dot(att.astype(BF16), v)
    return out, carry + (tail + ln)[:, 0:1]


_SB_TQ = 512
_SB_RQ = 256


def _sb_prompt_body(qi_ref, kj_ref, q_ref, k_ref, v_ref, bias_ref, o_ref, acc, carry, *, tq, rq):
    s = pl.program_id(1)
    qi = qi_ref[s]
    kj = kj_ref[s]
    later = _later_key_mask()

    @pl.when(kj == qi)
    def _():
        acc[...] = jnp.zeros_like(acc)
        carry[...] = jnp.zeros_like(carry)

    def sweep(diagonal):
        for h in range(N_HEAD):
            sl = slice(h * HEAD, (h + 1) * HEAD)
            bias = bias_ref[h]
            for qs in range(tq // rq):
                rows = slice(qs * rq, (qs + 1) * rq)
                q = (q_ref[rows, sl] * (HEAD ** -0.5)).astype(BF16)
                cr = carry[h, rows, :]
                a = acc[rows, sl]
                for ks in reversed(range(tq // LANE)):
                    visible = None
                    if diagonal:
                        if ks * LANE >= (qs + 1) * rq:
                            continue
                        if (ks + 1) * LANE > qs * rq:
                            row_id = qs * rq + lax.broadcasted_iota(jnp.int32, (rq, LANE), 0)
                            col_id = ks * LANE + lax.broadcasted_iota(jnp.int32, (rq, LANE), 1)
                            visible = col_id < row_id
                    keys = slice(ks * LANE, (ks + 1) * LANE)
                    out, cr = _sb_tile(q, k_ref[keys, sl].astype(BF16), v_ref[keys, sl].astype(BF16),
                                       bias, cr, later, visible)
                    a = a + out
                acc[rows, sl] = a
                carry[h, rows, :] = cr

    @pl.when(kj == qi)
    def _():
        sweep(True)

    @pl.when(kj != qi)
    def _():
        sweep(False)

    @pl.when(kj == 0)
    def _():
        o_ref[...] = acc[...]


def _sb_prompt(q, k, v, bias, *, n_seq, t_total):
    tq = min(_SB_TQ, t_total)
    rq = min(_SB_RQ, tq)
    nq = t_total // tq
    pairs = [(i, j) for i in range(nq) for j in range(i, -1, -1)]
    qi = jnp.asarray([p[0] for p in pairs], jnp.int32)
    kj = jnp.asarray([p[1] for p in pairs], jnp.int32)
    n = n_seq * t_total
    return pl.pallas_call(
        functools.partial(_sb_prompt_body, tq=tq, rq=rq),
        grid_spec=pltpu.PrefetchScalarGridSpec(
            num_scalar_prefetch=2,
            grid=(n_seq, len(pairs)),
            in_specs=[pl.BlockSpec((tq, W_A), lambda b, s, qi, kj: (b * nq + qi[s], 0)),
                      pl.BlockSpec((tq, W_A), lambda b, s, qi, kj: (b * nq + kj[s], 0)),
                      pl.BlockSpec((tq, W_A), lambda b, s, qi, kj: (b * nq + kj[s], 0)),
                      pl.BlockSpec(memory_space=pltpu.SMEM)],
            out_specs=pl.BlockSpec((tq, W_A), lambda b, s, qi, kj: (b * nq + qi[s], 0)),
            scratch_shapes=[pltpu.VMEM((tq, W_A), F32), pltpu.VMEM((N_HEAD, tq, 1), F32)]),
        out_shape=jax.ShapeDtypeStruct((n, W_A), F32),
        compiler_params=_cparams(("parallel", "arbitrary")),
        name="sb_prompt",
    )(qi, kj, q, k, v, bias)


_SB_PAGES = 8


def _sb_sample_body(pt_ref, q_ref, kn_ref, vn_ref, bias_ref, *refs):
    k_refs = refs[:_SB_PAGES]
    v_refs = refs[_SB_PAGES:2 * _SB_PAGES]
    o_ref, qx, bias_col, acc, carry = refs[2 * _SB_PAGES:]
    j = pl.program_id(1)
    m = N_HEAD * T_PAD
    later = _later_key_mask()
    row_head = lax.broadcasted_iota(jnp.int32, (m, 1), 0) // T_PAD

    def page(k, v, visible=None):
        out, cr = _sb_tile(qx[...], k.astype(BF16), v.astype(BF16), bias_col[...], carry[...], later, visible)
        acc[...] += out
        carry[...] = cr

    @pl.when(j == 0)
    def _():
        q = q_ref[0] * (HEAD ** -0.5)
        col_head = lax.broadcasted_iota(jnp.int32, (m, W_A), 1) // HEAD
        qx[...] = jnp.where(col_head == row_head, jnp.concatenate([q] * N_HEAD, axis=0), 0.0).astype(BF16)
        b = jnp.zeros((m, 1), F32)
        for h in range(N_HEAD):
            b = jnp.where(row_head == h, bias_ref[h], b)
        bias_col[...] = b
        acc[...] = jnp.zeros_like(acc)
        carry[...] = jnp.zeros_like(carry)
        tok = lax.broadcasted_iota(jnp.int32, (m, LANE), 0) % T_PAD
        key = lax.broadcasted_iota(jnp.int32, (m, LANE), 1)
        page(kn_ref[0], vn_ref[0], key < tok)

    for r in range(_SB_PAGES):
        page(k_refs[r][0], v_refs[r][0])

    @pl.when(j == pl.num_programs(1) - 1)
    def _():
        a = acc[...]
        col_head = lax.broadcasted_iota(jnp.int32, (T_PAD, W_A), 1) // HEAD
        o = jnp.zeros((T_PAD, W_A), F32)
        for h in range(N_HEAD):
            o = o + jnp.where(col_head == h, a[h * T_PAD:(h + 1) * T_PAD, :], 0.0)
        o_ref[0] = o


def _sb_sample(q, k_new, v_new, bias, k_cache, v_cache, page_table):
    n_seq, n_pages = page_table.shape
    pad = ((0, 0), (0, PAGE - T_PAD), (0, 0))
    kn = jnp.pad(k_new, pad)
    vn = jnp.pad(v_new, pad)
    seq_spec = lambda rows: pl.BlockSpec((1, rows, W_A), lambda b, j, pt: (b, 0, 0))

    def page_spec(r):
        return pl.BlockSpec((1, PAGE, W_A),
                            lambda b, j, pt: (pt[b * n_pages + n_pages - 1 - (j * _SB_PAGES + r)], 0, 0))

    m = N_HEAD * T_PAD
    return pl.pallas_call(
        _sb_sample_body,
        grid_spec=pltpu.PrefetchScalarGridSpec(
            num_scalar_prefetch=1,
            grid=(n_seq, n_pages // _SB_PAGES),
            in_specs=[seq_spec(T_PAD), seq_spec(PAGE), seq_spec(PAGE), pl.BlockSpec(memory_space=pltpu.SMEM)]
                     + [page_spec(r) for r in range(_SB_PAGES)] * 2,
            out_specs=seq_spec(T_PAD),
            scratch_shapes=[pltpu.VMEM((m, W_A), BF16), pltpu.VMEM((m, 1), F32),
                            pltpu.VMEM((m, W_A), F32), pltpu.VMEM((m, 1), F32)]),
        out_shape=jax.ShapeDtypeStruct((n_seq, T_PAD, W_A), F32),
        compiler_params=_cparams(("parallel", "arbitrary")),
        name="sb_sample",
    )(page_table.reshape(-1), q, kn, vn, bias, *([k_cache] * _SB_PAGES), *([v_cache] * _SB_PAGES))


def _mem_attn_body(q_ref, k_ref, v_ref, o_ref):
    for h in range(H_C):
        sl = slice(h * HD_C, (h + 1) * HD_C)
        s = _dot_nt(q_ref[:, sl].astype(BF16), k_ref[0, :, sl].astype(BF16)) * (HD_C ** -0.5)
        e = jnp.exp(s - jnp.max(s, axis=-1, keepdims=True))
        prob = e / jnp.sum(e, axis=-1, keepdims=True)
        o_ref[:, sl] = _dot(prob.astype(BF16), v_ref[0, :, sl].astype(BF16))


def _mem_attn(q, mem_k, mem_v, *, n_seq, t_total):
    tq = min(t_total, 512)
    nq = t_total // tq
    mem = pl.BlockSpec((1, N_MEM, W_C), lambda b, i: (b, 0, 0))
    row = pl.BlockSpec((tq, W_C), lambda b, i: (b * nq + i, 0))
    return pl.pallas_call(
        _mem_attn_body,
        grid=(n_seq, nq),
        in_specs=[row, mem, mem],
        out_specs=row,
        out_shape=jax.ShapeDtypeStruct((n_seq * t_total, W_C), F32),
        compiler_params=_cparams(("parallel", "parallel")),
        name="mem_attn",
    )(q, mem_k, mem_v)


def _combine_body(x_ref, oa_ref, ob_ref, oc_ref, gate_ref, wa_ref, wb_ref, wc_ref, wo_ref, g_ref, b_ref, y_ref):
    h = None
    for i, (o_ref, w_ref) in enumerate(((oa_ref, wa_ref), (ob_ref, wb_ref), (oc_ref, wc_ref))):
        part = _sigmoid(gate_ref[:, i * D_MODEL:(i + 1) * D_MODEL]) * _dot(o_ref[...].astype(BF16), w_ref[...])
        h = part if h is None else h + part
    y = DN_ALPHA * x_ref[...] + _dot(h.astype(BF16), wo_ref[...])
    y_ref[...] = _layer_norm(y, g_ref[...], b_ref[...])


def _combine(x, o_a, o_b, o_c, gate, wa, wb, wc, wo, ln_g, ln_b):
    n = x.shape[0]
    bm = min(n, 256)
    row = lambda w: pl.BlockSpec((bm, w), lambda i: (i, 0))
    const = lambda a: pl.BlockSpec(a.shape, lambda i: (0,) * a.ndim, pipeline_mode=pl.Buffered(1))
    return pl.pallas_call(
        _combine_body,
        grid=(n // bm,),
        in_specs=[row(D_MODEL), row(W_A), row(W_A), row(W_C), row(N_GATE)] + [const(a) for a in (wa, wb, wc, wo, ln_g, ln_b)],
        out_specs=row(D_MODEL),
        out_shape=jax.ShapeDtypeStruct((n, D_MODEL), F32),
        compiler_params=_cparams(("parallel",)),
        name="combine",
    )(x, o_a, o_b, o_c, gate, wa, wb, wc, wo, ln_g, ln_b)


def _top2(scores, ids, lowest):
    m1 = jnp.max(scores, axis=0, keepdims=True)
    i1 = jnp.min(jnp.where(scores == m1, ids, N_EXPERTS), axis=0, keepdims=True)
    rest = jnp.where(ids == i1, lowest, scores)
    m2 = jnp.max(rest, axis=0, keepdims=True)
    i2 = jnp.min(jnp.where(rest == m2, ids, N_EXPERTS), axis=0, keepdims=True)
    return m1, i1, m2, i2


def _router_gates(x, wr_t, br):
    m = x.shape[0]
    logits = _dot_f32(wr_t, x, 1, 1) + br
    e = jnp.exp(logits - jnp.max(logits, axis=0, keepdims=True))
    scores = e / jnp.sum(e, axis=0, keepdims=True)
    ids = lax.broadcasted_iota(jnp.int32, (N_EXPERTS, m), 0)
    grp = ids // E_PER_GROUP
    best = None
    for g in range(N_GROUPS):
        m1, _, m2, _ = _top2(jnp.where(grp == g, scores, -1.0), ids, -2.0)
        gs = m1 + m2
        if best is None:
            best, sel = gs, jnp.zeros((1, m), jnp.int32)
        else:
            sel = jnp.where(gs > best, g, sel)
            best = jnp.maximum(gs, best)
    m1, i1, m2, i2 = _top2(jnp.where(grp == sel, scores, -1.0), ids, -2.0)
    tot = m1 + m2
    return jnp.where(ids == i1, m1 / tot, 0.0) + jnp.where(ids == i2, m2 / tot, 0.0)


def _moe_body(x_ref, wr_ref, br_ref, w1_ref, w3_ref, w2_ref, g_ref, b_ref, y_ref, gate_scr, acc):
    e = pl.program_id(1)
    x = x_ref[...]

    @pl.when(e == 0)
    def _():
        gate_scr[...] = _router_gates(x, wr_ref[...], br_ref[...]).T
        acc[...] = jnp.zeros_like(acc)

    lane = lax.broadcasted_iota(jnp.int32, gate_scr.shape, 1)
    ge = jnp.sum(jnp.where(lane == e, gate_scr[...], 0.0), axis=1, keepdims=True)
    xb = x.astype(BF16)
    u = _dot(xb, w1_ref[0])
    h = (u * _sigmoid(u)) * _dot(xb, w3_ref[0])
    acc[...] += _dot((h * ge).astype(BF16), w2_ref[0])

    @pl.when(e == pl.num_programs(1) - 1)
    def _():
        y_ref[...] = _layer_norm(DN_ALPHA * x + acc[...], g_ref[...], b_ref[...])


def _moe(x, wr_t, br, w1, w3, w2, ln_g, ln_b):
    n = x.shape[0]
    bm = min(n, 1024)
    const = lambda a: pl.BlockSpec(a.shape, lambda i, e: (0,) * a.ndim)
    return pl.pallas_call(
        _moe_body,
        grid=(n // bm, N_EXPERTS),
        in_specs=[pl.BlockSpec((bm, D_MODEL), lambda i, e: (i, 0)), const(wr_t), const(br),
                  pl.BlockSpec((1, D_MODEL, D_FF), lambda i, e: (e, 0, 0)),
                  pl.BlockSpec((1, D_MODEL, D_FF), lambda i, e: (e, 0, 0)),
                  pl.BlockSpec((1, D_FF, D_MODEL), lambda i, e: (e, 0, 0)),
                  const(ln_g), const(ln_b)],
        out_specs=pl.BlockSpec((bm, D_MODEL), lambda i, e: (i, 0)),
        out_shape=jax.ShapeDtypeStruct((n, D_MODEL), F32),
        scratch_shapes=[pltpu.VMEM((bm, N_EXPERTS), F32), pltpu.VMEM((bm, D_MODEL), F32)],
        compiler_params=_cparams(("parallel", "arbitrary")),
        name="moe",
    )(x, wr_t, br, w1, w3, w2, ln_g, ln_b)


def _trunk_layer(x, lw, shared, sb_fn, mem_k, mem_v, wkv0, shift0, *, n_seq, t_total, t_valid):
    p_shift, q_b, k_b, v_b, q_c, gate = _in_proj(x, lw['w_in'])
    prep, shift = _rwkv_prep(p_shift, shift0, lw, n_seq=n_seq, t_total=t_total, t_valid=t_valid)
    o_a, wkv = _wkv(prep, wkv0, lw['gn_g'], lw['gn_b'], n_seq=n_seq, t_total=t_total)
    o_b = sb_fn(q_b, k_b, v_b)
    o_c = _mem_attn(q_c, mem_k, mem_v, n_seq=n_seq, t_total=t_total)
    x = _combine(x, o_a, o_b, o_c, gate, lw['w_br_a'], lw['w_br_b'], lw['w_br_c'], lw['w_out'], lw['ln1_g'], lw['ln1_b'])
    x = _moe(x, shared['wr_t'], shared['br'], lw['w1'], lw['w3'], lw['w2'], lw['ln2_g'], lw['ln2_b'])
    return x, k_b, v_b, wkv, shift


def kernel(x_prompt, x_sample, mem_prompt, cache_sb_k, cache_sb_v, cache_mem_k, cache_mem_v, state_wkv, state_shift, page_table, w_in, mu_shift, w0_decay, w_decay2, a0, w_a2, w_g2, k_k, k_a, r_k, gn_g, gn_b, sb_bias, w_mem_kv, w_br_a, w_br_b, w_br_c, w_out, ln1_g, ln1_b, w_router, b_router, w1, w3, w2, ln2_g, ln2_b):
    depth = w_in.shape[0]
    n_p, t_p = x_prompt.shape[:2]
    n_s, t_s = x_sample.shape[:2]
    n_pool = cache_sb_k.shape[1]
    row = lambda a, l: a[l].reshape(1, -1)
    shared = {'wr_t': w_router.T, 'br': b_router.reshape(-1, 1)}

    xp = x_prompt.reshape(n_p * t_p, D_MODEL)
    xs = jnp.pad(x_sample, ((0, 0), (0, T_PAD - t_s), (0, 0))).reshape(n_s * T_PAD, D_MODEL)
    mem_rows = mem_prompt.reshape(n_p * N_MEM, D_MODEL)
    zero_wkv = jnp.zeros((n_p, N_HEAD, HEAD, HEAD), F32)
    zero_shift = jnp.zeros((n_p, N_SHIFT), F32)

    outs = [[] for _ in range(10)]
    for l in range(depth):
        lw = {'w_in': w_in[l].astype(BF16), 'mu_shift': row(mu_shift, l), 'w0_decay': row(w0_decay, l),
              'w_decay2': w_decay2[l].astype(BF16), 'a0': row(a0, l), 'w_a2': w_a2[l].astype(BF16),
              'w_g2': w_g2[l].astype(BF16), 'k_k': row(k_k, l), 'k_a': row(k_a, l), 'r_k': row(r_k, l),
              'gn_g': row(gn_g, l), 'gn_b': row(gn_b, l),
              'w_br_a': w_br_a[l].astype(BF16), 'w_br_b': w_br_b[l].astype(BF16), 'w_br_c': w_br_c[l].astype(BF16),
              'w_out': w_out[l].astype(BF16), 'ln1_g': row(ln1_g, l), 'ln1_b': row(ln1_b, l),
              'w1': w1[l].astype(BF16), 'w3': w3[l].astype(BF16), 'w2': w2[l].astype(BF16),
              'ln2_g': row(ln2_g, l), 'ln2_b': row(ln2_b, l)}
        bias = sb_bias[l]

        mem_kv = _matmul(mem_rows, w_mem_kv[l].astype(BF16))
        mk = mem_kv[:, :W_C].reshape(n_p, N_MEM, W_C)
        mv = mem_kv[:, W_C:].reshape(n_p, N_MEM, W_C)
        sb_p = functools.partial(_sb_prompt, bias=bias, n_seq=n_p, t_total=t_p)
        xp, kb, vb, wkv, sh = _trunk_layer(xp, lw, shared, lambda q, k, v: sb_p(q, k, v), mk, mv, zero_wkv, zero_shift,
                                           n_seq=n_p, t_total=t_p, t_valid=t_p)
        for i, a in zip((0, 1, 4, 5, 6, 8), (kb.reshape(n_p, t_p, N_HEAD, HEAD), vb.reshape(n_p, t_p, N_HEAD, HEAD),
                                             mk.reshape(n_p, N_MEM, H_C, HD_C), mv.reshape(n_p, N_MEM, H_C, HD_C), wkv, sh)):
            outs[i].append(a)

        kc = cache_sb_k[l].reshape(n_pool, PAGE, W_A)
        vc = cache_sb_v[l].reshape(n_pool, PAGE, W_A)

        def sb_s(q, k, v):
            shp = (n_s, T_PAD, W_A)
            return _sb_sample(q.reshape(shp), k.reshape(shp), v.reshape(shp), bias, kc, vc, page_table).reshape(-1, W_A)

        xs, kb, vb, wkv, sh = _trunk_layer(xs, lw, shared, sb_s, cache_mem_k[l].reshape(n_s, N_MEM, W_C),
                                           cache_mem_v[l].reshape(n_s, N_MEM, W_C), state_wkv[l], state_shift[l],
                                           n_seq=n_s, t_total=T_PAD, t_valid=t_s)
        for i, a in zip((2, 3, 7, 9), (kb.reshape(n_s, T_PAD, N_HEAD, HEAD)[:, :t_s], vb.reshape(n_s, T_PAD, N_HEAD, HEAD)[:, :t_s],
                                       wkv, sh)):
            outs[i].append(a)

    st = [jnp.stack(o) for o in outs]
    y_p = xp.reshape(n_p, t_p, D_MODEL)
    y_s = xs.reshape(n_s, T_PAD, D_MODEL)[:, :t_s]
    return (y_p, y_s, st[0], st[1], st[2], st[3], st[4], st[5], st[6], st[7], st[8], st[9])
```

```python
import functools

import numpy as np
import jax
import jax.numpy as jnp
from jax import lax
from jax.experimental import pallas as pl
from jax.experimental.pallas import tpu as pltpu

F32 = jnp.float32
BF16 = jnp.bfloat16

D_MODEL = 1024
DEPTH = 4
N_HEAD = 8
HEAD = 64
W_A = N_HEAD * HEAD
R_DECAY, R_AAA, R_GATE = 64, 64, 128
H_C, HD_C = 4, 128
W_C = H_C * HD_C
N_MEM = 256
N_SHIFT = 3 * W_A + R_DECAY + R_AAA + R_GATE
N_GATE = 3 * D_MODEL
N_IN = N_SHIFT + 3 * W_A + W_C + N_GATE
N_EXPERTS, N_GROUPS, E_PER_GROUP, D_FF = 16, 4, 4, 512
DN_ALPHA = (2 * DEPTH) ** 0.25
LN_EPS = 1e-5
GN_EPS = 64e-5
NORM_EPS = 1e-12
PAGE = 128
LANE = 128
SUBLANE = 8
T_PAD = SUBLANE
VMEM_LIMIT = 56 * 1024 * 1024


def _cparams(sem):
    return pltpu.CompilerParams(dimension_semantics=sem, vmem_limit_bytes=VMEM_LIMIT)


def _softplus(z):
    return jnp.maximum(z, 0.0) + jnp.log1p(jnp.exp(-jnp.abs(z)))


def _sigmoid(z):
    return 1.0 / (1.0 + jnp.exp(-z))


def _dot(a, b):
    return jnp.dot(a, b, preferred_element_type=F32)


def _dot_nt(a, b):
    return lax.dot_general(a, b, (((1,), (1,)), ((), ())), preferred_element_type=F32)


def _bf16_pieces(a, n):
    pieces = []
    for _ in range(n):
        p = a.astype(BF16)
        pieces.append(p)
        a = a - p.astype(F32)
    return pieces


def _dot_f32(a, b, ca=1, cb=0):
    ah, al = _bf16_pieces(a, 2)
    bh, bl = _bf16_pieces(b, 2)
    return lax.dot_general(jnp.concatenate([ah, ah, al], axis=ca), jnp.concatenate([bh, bl, bh], axis=cb),
                           (((ca,), (cb,)), ((), ())), preferred_element_type=F32)


def _dot_mask_f32(mask, b):
    m = mask.astype(BF16)
    return _dot(jnp.concatenate([m, m, m], axis=1), jnp.concatenate(_bf16_pieces(b, 3), axis=0))


def _layer_norm(y, g, b):
    mu = jnp.mean(y, axis=-1, keepdims=True)
    yc = y - mu
    var = jnp.mean(yc * yc, axis=-1, keepdims=True)
    return yc * lax.rsqrt(var + LN_EPS) * g + b


_MM_COLS = 256
_SB_SCALE = HEAD ** -0.5
_IN_GROUPS = ((N_SHIFT, True, False, 1.0),
              (W_A, False, True, _SB_SCALE),
              (W_A, True, True, 1.0),
              (W_A, True, True, 1.0),
              (W_C, True, False, 1.0),
              (N_GATE, True, False, 1.0))


def _in_proj_out_types(n):
    types = []
    for width, want_f32, want_bf16, _ in _IN_GROUPS:
        types += [(width, F32)] * want_f32 + [(width, BF16)] * want_bf16
    return types


def _in_proj_body(x_ref, w_ref, *out_refs):
    x = x_ref[...].astype(BF16)
    refs = iter(out_refs)
    off = 0
    for width, want_f32, want_bf16, scale in _IN_GROUPS:
        ref_f32 = next(refs) if want_f32 else None
        ref_bf16 = next(refs) if want_bf16 else None
        for c in range(0, width, _MM_COLS):
            cols = slice(c, c + _MM_COLS)
            y = _dot(x, w_ref[:, off + c:off + c + _MM_COLS])
            if want_f32:
                ref_f32[:, cols] = y
            if want_bf16:
                ref_bf16[:, cols] = (y * scale).astype(BF16)
        off += width


def _in_proj(x, w_in_bf16):
    n = x.shape[0]
    bm = min(n, 256)
    types = _in_proj_out_types(n)
    return pl.pallas_call(
        _in_proj_body,
        grid=(n // bm,),
        in_specs=[pl.BlockSpec((bm, D_MODEL), lambda i: (i, 0)),
                  pl.BlockSpec((D_MODEL, N_IN), lambda i: (0, 0), pipeline_mode=pl.Buffered(1))],
        out_specs=[pl.BlockSpec((bm, w), lambda i: (i, 0)) for w, _ in types],
        out_shape=[jax.ShapeDtypeStruct((n, w), dt) for w, dt in types],
        compiler_params=_cparams(("parallel",)),
        name="in_proj",
    )(x, w_in_bf16)


def _matmul_body(x_ref, w_ref, o_ref):
    o_ref[...] = _dot(x_ref[...].astype(BF16), w_ref[...])


def _matmul(x, w_bf16, bn=512):
    n, k = x.shape
    m = w_bf16.shape[1]
    bm = min(n, 256)
    return pl.pallas_call(
        _matmul_body,
        grid=(n // bm, m // bn),
        in_specs=[pl.BlockSpec((bm, k), lambda i, j: (i, 0)), pl.BlockSpec((k, bn), lambda i, j: (0, j))],
        out_specs=pl.BlockSpec((bm, bn), lambda i, j: (i, j)),
        out_shape=jax.ShapeDtypeStruct((n, m), F32),
        compiler_params=_cparams(("parallel", "parallel")),
        name="matmul",
    )(x, w_bf16)


def _head_sum(x):
    n = x.shape[-1]
    same = (lax.broadcasted_iota(jnp.int32, (n, n), 0) // HEAD) == (lax.broadcasted_iota(jnp.int32, (n, n), 1) // HEAD)
    same = same.astype(BF16)
    return _dot(jnp.concatenate(_bf16_pieces(x, 3), axis=1), jnp.concatenate([same, same, same], axis=0))


def _rwkv_prep_body(p_ref, sh0_ref, mu_ref, w0_ref, wd2_ref, a0_ref, wa2_ref, wg2_ref, kk_ref, ka_ref, rk_ref,
                    r_o, lw_o, k_o, v_o, kn_o, bb_o, g_o, bonus_o, shift_o, carry, *, tb, t_valid, t_total):
    t = pl.program_id(1)

    @pl.when(t == 0)
    def _():
        carry[...] = sh0_ref[0]

    p = p_ref[...]
    row = lax.broadcasted_iota(jnp.int32, (tb, 1), 0)
    prev = jnp.where(row == 0, carry[...], pltpu.roll(p, 1, 0))
    carry[...] = p[tb - 1:tb, :]

    last = t_valid - 1
    @pl.when(t == last // tb)
    def _():
        shift_o[0] = p[last % tb:last % tb + 1, :]

    z = p + mu_ref[...] * (prev - p)
    r = z[:, 0:W_A]
    k = z[:, W_A:2 * W_A]
    v = z[:, 2 * W_A:3 * W_A]
    c0 = 3 * W_A
    wl = z[:, c0:c0 + R_DECAY]
    al = z[:, c0 + R_DECAY:c0 + R_DECAY + R_AAA]
    gl = z[:, c0 + R_DECAY + R_AAA:N_SHIFT]

    w_log = -_softplus(-(w0_ref[...] + _dot(jnp.tanh(wl).astype(BF16), wd2_ref[...]))) - 0.5
    lw = -jnp.exp(w_log)
    a = _sigmoid(a0_ref[...] + _dot(al.astype(BF16), wa2_ref[...]))
    g = _dot(_sigmoid(gl).astype(BF16), wg2_ref[...])
    kn = k * kk_ref[...]
    kn = kn / jnp.maximum(jnp.sqrt(_head_sum(kn * kn)), NORM_EPS)
    k2 = k * (1.0 + (a - 1.0) * ka_ref[...])
    bonus = _head_sum(r * k2 * rk_ref[...]) * v
    bb = kn * a
    if t_valid < t_total:
        ok = (t * tb + row) < t_valid
        lw = jnp.where(ok, lw, 0.0)
        kn = jnp.where(ok, kn, 0.0)
        bb = jnp.where(ok, bb, 0.0)
        k2 = jnp.where(ok, k2, 0.0)
        v = jnp.where(ok, v, 0.0)
    r_o[...] = r
    lw_o[...] = lw
    k_o[...] = k2
    v_o[...] = v
    kn_o[...] = kn
    bb_o[...] = bb
    g_o[...] = g
    bonus_o[...] = bonus


def _rwkv_prep(p_shift, shift0, lw, *, n_seq, t_total, t_valid):
    tb = min(t_total, 256)
    nt = t_total // tb
    n = n_seq * t_total
    row_spec = lambda w: pl.BlockSpec((tb, w), lambda b, t: (b * nt + t, 0))
    par = lambda a: pl.BlockSpec(a.shape, lambda b, t: (0,) * a.ndim)
    params = (lw['mu_shift'], lw['w0_decay'], lw['w_decay2'], lw['a0'], lw['w_a2'], lw['w_g2'],
              lw['k_k'], lw['k_a'], lw['r_k'])
    outs = pl.pallas_call(
        functools.partial(_rwkv_prep_body, tb=tb, t_valid=t_valid, t_total=t_total),
        grid=(n_seq, nt),
        in_specs=[row_spec(N_SHIFT), pl.BlockSpec((1, 1, N_SHIFT), lambda b, t: (b, 0, 0))] + [par(a) for a in params],
        out_specs=[row_spec(W_A)] * 8 + [pl.BlockSpec((1, 1, N_SHIFT), lambda b, t: (b, 0, 0))],
        out_shape=[jax.ShapeDtypeStruct((n, W_A), F32)] * 8 + [jax.ShapeDtypeStruct((n_seq, 1, N_SHIFT), F32)],
        scratch_shapes=[pltpu.VMEM((1, N_SHIFT), F32)],
        compiler_params=_cparams(("parallel", "arbitrary")),
        name="rwkv_prep",
    )(p_shift, shift0.reshape(n_seq, 1, N_SHIFT), *params)
    return outs[:8], outs[8].reshape(n_seq, N_SHIFT)


_INV_BLOCK = 16


def _nilpotent_inverses(ns, steps):
    invs = [-n for n in ns]
    sqs = list(ns)
    for _ in range(steps - 1):
        sqs = [_dot_f32(q, q) for q in sqs]
        invs = [i + q + _dot_f32(i, q) for i, q in zip(invs, sqs)]
    return invs


def _unit_lower_inverses_minus_i(ns, c):
    blk = min(_INV_BLOCK, c)
    if blk == c:
        return _nilpotent_inverses(ns, int(np.log2(c)))
    ri = lax.broadcasted_iota(jnp.int32, (c, c), 0)
    ci = lax.broadcasted_iota(jnp.int32, (c, c), 1)
    on_diag = (ri // blk) == (ci // blk)
    eye = (ri == ci).astype(F32)
    d_invs = [d + eye for d in _nilpotent_inverses([jnp.where(on_diag, n, 0.0) for n in ns], int(np.log2(blk)))]
    ms = [_dot_f32(d, jnp.where(on_diag, 0.0, n)) for d, n in zip(d_invs, ns)]
    m_invs = [m + eye for m in _nilpotent_inverses(ms, int(np.log2(c // blk)))]
    return [_dot_f32(m, d) - eye for m, d in zip(m_invs, d_invs)]


def _wkv_body(r_ref, lw_ref, k_ref, v_ref, kn_ref, bb_ref, g_ref, bonus_ref, s0_ref, gng_ref, gnb_ref,
              o_ref, sT_ref, state, o_scr, *, c):
    ci = pl.program_id(1)

    @pl.when(ci == 0)
    def _():
        state[...] = s0_ref[0]

    ri = lax.broadcasted_iota(jnp.int32, (c, c), 0)
    cj = lax.broadcasted_iota(jnp.int32, (c, c), 1)
    incl = ri >= cj
    strict = ri > cj

    lw = lw_ref[...]
    cs = _dot_mask_f32(incl, lw)
    tot = cs[c - 1:c, :]
    e_out = jnp.exp(-cs)
    e_end = jnp.exp(tot - cs)
    k = k_ref[...]
    bb = bb_ref[...]
    r_t = r_ref[...] * jnp.exp(cs)
    kap = kn_ref[...] * jnp.exp(cs - lw)
    k_t = k * e_out
    b_t = bb * e_out
    k_d = k * e_end
    b_d = bb * e_end
    e_tot = jnp.exp(tot)
    v = v_ref[...]

    heads = [slice(h * HEAD, (h + 1) * HEAD) for h in range(N_HEAD)]
    kap_h = [kap[:, sl] for sl in heads]
    v_h = [v[:, sl] for sl in heads]
    kr = [jnp.concatenate([kp, r_t[:, sl]], axis=0) for kp, sl in zip(kap_h, heads)]
    g_k = [_dot_f32(x, k_t[:, sl], 1, 1) for x, sl in zip(kr, heads)]
    g_b = [_dot_f32(x, b_t[:, sl], 1, 1) for x, sl in zip(kr, heads)]
    a_kk = [jnp.where(strict, g[:c], 0.0) for g in g_k]
    a_kr = [jnp.where(incl, g[c:], 0.0) for g in g_k]
    a_bk = [jnp.where(strict, g[:c], 0.0) for g in g_b]
    a_br = [jnp.where(incl, g[c:], 0.0) for g in g_b]
    t_inv = _unit_lower_inverses_minus_i(a_bk, c)
    rhs0 = [_dot_f32(a, x) for a, x in zip(a_kk, v_h)]
    u0 = [x + _dot_f32(t, x) for t, x in zip(t_inv, rhs0)]
    w1 = [x + _dot_f32(t, x) for t, x in zip(t_inv, kap_h)]
    s_h = [state[h] for h in range(N_HEAD)]
    su = [_dot_f32(jnp.concatenate([w, r_t[:, sl]], axis=0), s, 1, 1) for w, sl, s in zip(w1, heads, s_h)]
    u = [x[:c] + y for x, y in zip(su, u0)]
    o = [x[c:] + _dot_f32(a, y) - _dot_f32(b, z) for x, a, y, b, z in zip(su, a_kr, v_h, a_br, u)]
    for h, sl in enumerate(heads):
        vu = jnp.concatenate([v_h[h], -u[h]], axis=0)
        kb = jnp.concatenate([k_d[:, sl], b_d[:, sl]], axis=0)
        state[h] = s_h[h] * e_tot[:, sl] + _dot_f32(vu, kb, 0, 0)
    for h, sl in enumerate(heads):
        mu = jnp.mean(o[h], axis=-1, keepdims=True)
        oc = o[h] - mu
        var = jnp.mean(oc * oc, axis=-1, keepdims=True)
        o_scr[:, sl] = oc * lax.rsqrt(var + GN_EPS)

    o_ref[...] = (o_scr[...] * gng_ref[...] + gnb_ref[...] + bonus_ref[...]) * g_ref[...]

    @pl.when(ci == pl.num_programs(1) - 1)
    def _():
        sT_ref[0] = state[...]


def _wkv(prep, state0, gn_g, gn_b, *, n_seq, t_total):
    c = min(t_total, 64)
    nc = t_total // c
    n = n_seq * t_total
    row = pl.BlockSpec((c, W_A), lambda b, i: (b * nc + i, 0))
    st = pl.BlockSpec((1, N_HEAD, HEAD, HEAD), lambda b, i: (b, 0, 0, 0))
    par = pl.BlockSpec((1, W_A), lambda b, i: (0, 0))
    return pl.pallas_call(
        functools.partial(_wkv_body, c=c),
        grid=(n_seq, nc),
        in_specs=[row] * 8 + [st, par, par],
        out_specs=[row, st],
        out_shape=[jax.ShapeDtypeStruct((n, W_A), F32), jax.ShapeDtypeStruct((n_seq, N_HEAD, HEAD, HEAD), F32)],
        scratch_shapes=[pltpu.VMEM((N_HEAD, HEAD, HEAD), F32), pltpu.VMEM((c, W_A), F32)],
        compiler_params=_cparams(("parallel", "arbitrary")),
        name="wkv",
    )(*prep, state0, gn_g, gn_b)


def _later_key_mask(n):
    j = lax.broadcasted_iota(jnp.int32, (n, n), 0)
    s = lax.broadcasted_iota(jnp.int32, (n, n), 1)
    return jnp.where(j > s, -1.0, 0.0).astype(BF16)


_SP_LINEAR = 20.0


def _sb_scores(z, carry, later, visible=None):
    sp = jnp.where(z > _SP_LINEAR, z, jnp.log(1.0 + jnp.exp(jnp.minimum(z, _SP_LINEAR))))
    if visible is not None:
        sp = jnp.where(visible, sp, 0.0)
    tail = _dot(sp.astype(BF16), later)
    att = jnp.exp(z - sp + tail + carry)
    if visible is not None:
        att = jnp.where(visible, att, 0.0)
    return att, carry + (tail - sp)[:, 0:1]


_SB_TQ = 512
_SB_RQ = 512
_SB_KW = 256


def _sb_prompt_body(qi_ref, kj_ref, q_ref, k_ref, v_ref, bias_ref, o_ref, acc, carry, *, tq, rq, kw):
    s = pl.program_id(1)
    qi = qi_ref[s]
    kj = kj_ref[s]
    later = _later_key_mask(kw)

    @pl.when(kj == qi)
    def _():
        acc[...] = jnp.zeros_like(acc)
        carry[...] = jnp.zeros_like(carry)

    def sweep(diagonal):
        masks = {}

        def visible_mask(qs, ks):
            if (qs, ks) not in masks:
                row_id = qs * rq + lax.broadcasted_iota(jnp.int32, (rq, kw), 0)
                col_id = ks * kw + lax.broadcasted_iota(jnp.int32, (rq, kw), 1)
                masks[qs, ks] = col_id < row_id
            return masks[qs, ks]

        for h in range(N_HEAD):
            sl = slice(h * HEAD, (h + 1) * HEAD)
            bias = bias_ref[h]
            for qs in range(tq // rq):
                rows = slice(qs * rq, (qs + 1) * rq)
                q = q_ref[rows, sl]
                cr = carry[h, rows, :]
                a = acc[rows, sl]
                for ks in reversed(range(tq // kw)):
                    visible = None
                    if diagonal:
                        if ks * kw >= (qs + 1) * rq:
                            continue
                        if (ks + 1) * kw > qs * rq:
                            visible = visible_mask(qs, ks)
                    keys = slice(ks * kw, (ks + 1) * kw)
                    att, cr = _sb_scores(_dot_nt(q, k_ref[keys, sl]) + bias, cr, later, visible)
                    a = a + _dot(att.astype(BF16), v_ref[keys, sl])
                acc[rows, sl] = a
                carry[h, rows, :] = cr

    @pl.when(kj == qi)
    def _():
        sweep(True)

    @pl.when(kj != qi)
    def _():
        sweep(False)

    @pl.when(kj == 0)
    def _():
        o_ref[...] = acc[...]


def _sb_prompt(q, k, v, bias, *, n_seq, t_total):
    tq = min(_SB_TQ, t_total)
    rq = min(_SB_RQ, tq)
    kw = min(_SB_KW, tq)
    nq = t_total // tq
    pairs = [(i, j) for i in range(nq) for j in range(i, -1, -1)]
    qi = jnp.asarray([p[0] for p in pairs], jnp.int32)
    kj = jnp.asarray([p[1] for p in pairs], jnp.int32)
    n = n_seq * t_total
    return pl.pallas_call(
        functools.partial(_sb_prompt_body, tq=tq, rq=rq, kw=kw),
        grid_spec=pltpu.PrefetchScalarGridSpec(
            num_scalar_prefetch=2,
            grid=(n_seq, len(pairs)),
            in_specs=[pl.BlockSpec((tq, W_A), lambda b, s, qi, kj: (b * nq + qi[s], 0)),
                      pl.BlockSpec((tq, W_A), lambda b, s, qi, kj: (b * nq + kj[s], 0)),
                      pl.BlockSpec((tq, W_A), lambda b, s, qi, kj: (b * nq + kj[s], 0)),
                      pl.BlockSpec(memory_space=pltpu.SMEM)],
            out_specs=pl.BlockSpec((tq, W_A), lambda b, s, qi, kj: (b * nq + qi[s], 0)),
            scratch_shapes=[pltpu.VMEM((tq, W_A), F32), pltpu.VMEM((N_HEAD, tq, 1), F32)]),
        out_shape=jax.ShapeDtypeStruct((n, W_A), F32),
        compiler_params=_cparams(("parallel", "arbitrary")),
        name="sb_prompt",
    )(qi, kj, q, k, v, bias)


_SB_PAGES = 8


def _sb_sample_body(pt_ref, q_ref, kn_ref, vn_ref, bias_ref, *refs):
    k_refs = refs[:_SB_PAGES]
    v_refs = refs[_SB_PAGES:2 * _SB_PAGES]
    o_ref, bias_col, acc, carry = refs[2 * _SB_PAGES:]
    j = pl.program_id(1)
    m = N_HEAD * T_PAD
    later = _later_key_mask(PAGE)

    def head_rows(ref, h):
        return ref[0, pl.ds(h, PAGE, stride=N_HEAD), :].astype(BF16)

    def page(k_ref, v_ref, visible=None):
        z = jnp.concatenate([_dot_nt(q_ref[0, h].astype(BF16), head_rows(k_ref, h)) for h in range(N_HEAD)], axis=0)
        att, cr = _sb_scores(z + bias_col[...], carry[...], later, visible)
        carry[...] = cr
        for h in range(N_HEAD):
            acc[h] += _dot(att[h * T_PAD:(h + 1) * T_PAD, :].astype(BF16), head_rows(v_ref, h))

    @pl.when(j == 0)
    def _():
        row_head = lax.broadcasted_iota(jnp.int32, (m, 1), 0) // T_PAD
        b = jnp.zeros((m, 1), F32)
        for h in range(N_HEAD):
            b = jnp.where(row_head == h, bias_ref[h], b)
        bias_col[...] = b
        acc[...] = jnp.zeros_like(acc)
        carry[...] = jnp.zeros_like(carry)
        tok = lax.broadcasted_iota(jnp.int32, (m, PAGE), 0) % T_PAD
        key = lax.broadcasted_iota(jnp.int32, (m, PAGE), 1)
        page(kn_ref, vn_ref, key < tok)

    for r in range(_SB_PAGES):
        page(k_refs[r], v_refs[r])

    @pl.when(j == pl.num_programs(1) - 1)
    def _():
        for h in range(N_HEAD):
            o_ref[0, :, h * HEAD:(h + 1) * HEAD] = acc[h]


def _sb_sample(q, k_new, v_new, bias, k_cache, v_cache, page_table, layer):
    n_seq, n_pages = page_table.shape
    depth, n_pool = k_cache.shape[:2]
    rows = PAGE * N_HEAD
    as_pages = lambda c: c.reshape(depth * n_pool, rows, HEAD)
    q4 = q.reshape(n_seq, T_PAD, N_HEAD, HEAD).transpose(0, 2, 1, 3)

    def new_page(x):
        x = x.reshape(n_seq, T_PAD, N_HEAD, HEAD)
        return jnp.pad(x, ((0, 0), (0, PAGE - T_PAD), (0, 0), (0, 0))).reshape(n_seq, rows, HEAD)

    seq_page = pl.BlockSpec((1, rows, HEAD), lambda b, j, pt: (b, 0, 0))

    def page_spec(r):
        return pl.BlockSpec((1, rows, HEAD),
                            lambda b, j, pt: (layer * n_pool + pt[b * n_pages + n_pages - 1 - (j * _SB_PAGES + r)], 0, 0))

    m = N_HEAD * T_PAD
    out = pl.pallas_call(
        _sb_sample_body,
        grid_spec=pltpu.PrefetchScalarGridSpec(
            num_scalar_prefetch=1,
            grid=(n_seq, n_pages // _SB_PAGES),
            in_specs=[pl.BlockSpec((1, N_HEAD, T_PAD, HEAD), lambda b, j, pt: (b, 0, 0, 0)), seq_page, seq_page,
                      pl.BlockSpec(memory_space=pltpu.SMEM)] + [page_spec(r) for r in range(_SB_PAGES)] * 2,
            out_specs=pl.BlockSpec((1, T_PAD, W_A), lambda b, j, pt: (b, 0, 0)),
            scratch_shapes=[pltpu.VMEM((m, 1), F32), pltpu.VMEM((N_HEAD, T_PAD, HEAD), F32), pltpu.VMEM((m, 1), F32)]),
        out_shape=jax.ShapeDtypeStruct((n_seq, T_PAD, W_A), F32),
        compiler_params=_cparams(("parallel", "arbitrary")),
        name="sb_sample",
    )(page_table.reshape(-1), q4, new_page(k_new), new_page(v_new), bias,
      *([as_pages(k_cache)] * _SB_PAGES), *([as_pages(v_cache)] * _SB_PAGES))
    return out.reshape(n_seq * T_PAD, W_A)


def _mem_attn_body(q_ref, k_ref, v_ref, o_ref):
    for h in range(H_C):
        sl = slice(h * HD_C, (h + 1) * HD_C)
        s = _dot_nt(q_ref[:, sl].astype(BF16), k_ref[0, :, sl].astype(BF16)) * (HD_C ** -0.5)
        e = jnp.exp(s - jnp.max(s, axis=-1, keepdims=True))
        prob = e / jnp.sum(e, axis=-1, keepdims=True)
        o_ref[:, sl] = _dot(prob.astype(BF16), v_ref[0, :, sl].astype(BF16))


def _mem_attn(q, mem_k, mem_v, *, n_seq, t_total):
    tq = min(t_total, 512)
    nq = t_total // tq
    mem = pl.BlockSpec((1, N_MEM, W_C), lambda b, i: (b, 0, 0))
    row = pl.BlockSpec((tq, W_C), lambda b, i: (b * nq + i, 0))
    return pl.pallas_call(
        _mem_attn_body,
        grid=(n_seq, nq),
        in_specs=[row, mem, mem],
        out_specs=row,
        out_shape=jax.ShapeDtypeStruct((n_seq * t_total, W_C), F32),
        compiler_params=_cparams(("parallel", "parallel")),
        name="mem_attn",
    )(q, mem_k, mem_v)


def _combine_body(x_ref, oa_ref, ob_ref, oc_ref, gate_ref, wa_ref, wb_ref, wc_ref, wo_ref, g_ref, b_ref, y_ref):
    h = None
    for i, (o_ref, w_ref) in enumerate(((oa_ref, wa_ref), (ob_ref, wb_ref), (oc_ref, wc_ref))):
        part = _sigmoid(gate_ref[:, i * D_MODEL:(i + 1) * D_MODEL]) * _dot(o_ref[...].astype(BF16), w_ref[...])
        h = part if h is None else h + part
    y = DN_ALPHA * x_ref[...] + _dot(h.astype(BF16), wo_ref[...])
    y_ref[...] = _layer_norm(y, g_ref[...], b_ref[...])


def _combine(x, o_a, o_b, o_c, gate, wa, wb, wc, wo, ln_g, ln_b):
    n = x.shape[0]
    bm = min(n, 256)
    row = lambda w: pl.BlockSpec((bm, w), lambda i: (i, 0))
    const = lambda a: pl.BlockSpec(a.shape, lambda i: (0,) * a.ndim, pipeline_mode=pl.Buffered(1))
    return pl.pallas_call(
        _combine_body,
        grid=(n // bm,),
        in_specs=[row(D_MODEL), row(W_A), row(W_A), row(W_C), row(N_GATE)] + [const(a) for a in (wa, wb, wc, wo, ln_g, ln_b)],
        out_specs=row(D_MODEL),
        out_shape=jax.ShapeDtypeStruct((n, D_MODEL), F32),
        compiler_params=_cparams(("parallel",)),
        name="combine",
    )(x, o_a, o_b, o_c, gate, wa, wb, wc, wo, ln_g, ln_b)


def _top2(scores, ids, lowest):
    m1 = jnp.max(scores, axis=0, keepdims=True)
    i1 = jnp.min(jnp.where(scores == m1, ids, N_EXPERTS), axis=0, keepdims=True)
    rest = jnp.where(ids == i1, lowest, scores)
    m2 = jnp.max(rest, axis=0, keepdims=True)
    i2 = jnp.min(jnp.where(rest == m2, ids, N_EXPERTS), axis=0, keepdims=True)
    return m1, i1, m2, i2


def _router_gates(x, wr_t, br):
    m = x.shape[0]
    logits = _dot_f32(wr_t, x, 1, 1) + br
    e = jnp.exp(logits - jnp.max(logits, axis=0, keepdims=True))
    scores = e / jnp.sum(e, axis=0, keepdims=True)
    ids = lax.broadcasted_iota(jnp.int32, (N_EXPERTS, m), 0)
    grp = ids // E_PER_GROUP
    best = None
    for g in range(N_GROUPS):
        m1, _, m2, _ = _top2(jnp.where(grp == g, scores, -1.0), ids, -2.0)
        gs = m1 + m2
        if best is None:
            best, sel = gs, jnp.zeros((1, m), jnp.int32)
        else:
            sel = jnp.where(gs > best, g, sel)
            best = jnp.maximum(gs, best)
    m1, i1, m2, i2 = _top2(jnp.where(grp == sel, scores, -1.0), ids, -2.0)
    tot = m1 + m2
    return jnp.where(ids == i1, m1 / tot, 0.0) + jnp.where(ids == i2, m2 / tot, 0.0)


def _moe_body(x_ref, wr_ref, br_ref, w1_ref, w3_ref, w2_ref, g_ref, b_ref, y_ref, gate_scr, acc):
    e = pl.program_id(1)
    x = x_ref[...]

    @pl.when(e == 0)
    def _():
        gate_scr[...] = _router_gates(x, wr_ref[...], br_ref[...]).T
        acc[...] = jnp.zeros_like(acc)

    lane = lax.broadcasted_iota(jnp.int32, gate_scr.shape, 1)
    ge = jnp.sum(jnp.where(lane == e, gate_scr[...], 0.0), axis=1, keepdims=True)
    xb = x.astype(BF16)
    u = _dot(xb, w1_ref[0])
    h = (u * _sigmoid(u)) * _dot(xb, w3_ref[0])
    acc[...] += _dot((h * ge).astype(BF16), w2_ref[0])

    @pl.when(e == pl.num_programs(1) - 1)
    def _():
        y_ref[...] = _layer_norm(DN_ALPHA * x + acc[...], g_ref[...], b_ref[...])


def _moe(x, wr_t, br, w1, w3, w2, ln_g, ln_b):
    n = x.shape[0]
    bm = min(n, 1024)
    const = lambda a: pl.BlockSpec(a.shape, lambda i, e: (0,) * a.ndim)
    return pl.pallas_call(
        _moe_body,
        grid=(n // bm, N_EXPERTS),
        in_specs=[pl.BlockSpec((bm, D_MODEL), lambda i, e: (i, 0)), const(wr_t), const(br),
                  pl.BlockSpec((1, D_MODEL, D_FF), lambda i, e: (e, 0, 0)),
                  pl.BlockSpec((1, D_MODEL, D_FF), lambda i, e: (e, 0, 0)),
                  pl.BlockSpec((1, D_FF, D_MODEL), lambda i, e: (e, 0, 0)),
                  const(ln_g), const(ln_b)],
        out_specs=pl.BlockSpec((bm, D_MODEL), lambda i, e: (i, 0)),
        out_shape=jax.ShapeDtypeStruct((n, D_MODEL), F32),
        scratch_shapes=[pltpu.VMEM((bm, N_EXPERTS), F32), pltpu.VMEM((bm, D_MODEL), F32)],
        compiler_params=_cparams(("parallel", "arbitrary")),
        name="moe",
    )(x, wr_t, br, w1, w3, w2, ln_g, ln_b)


def _trunk_layer(x, lw, shared, sb_fn, mem_k, mem_v, wkv0, shift0, *, n_seq, t_total, t_valid):
    p_shift, q16, k_b, k16, v_b, v16, q_c, gate = _in_proj(x, lw['w_in'])
    prep, shift = _rwkv_prep(p_shift, shift0, lw, n_seq=n_seq, t_total=t_total, t_valid=t_valid)
    o_a, wkv = _wkv(prep, wkv0, lw['gn_g'], lw['gn_b'], n_seq=n_seq, t_total=t_total)
    o_b = sb_fn(q16, k_b, k16, v_b, v16)
    o_c = _mem_attn(q_c, mem_k, mem_v, n_seq=n_seq, t_total=t_total)
    x = _combine(x, o_a, o_b, o_c, gate, lw['w_br_a'], lw['w_br_b'], lw['w_br_c'], lw['w_out'], lw['ln1_g'], lw['ln1_b'])
    x = _moe(x, shared['wr_t'], shared['br'], lw['w1'], lw['w3'], lw['w2'], lw['ln2_g'], lw['ln2_b'])
    return x, k_b, v_b, wkv, shift


def kernel(x_prompt, x_sample, mem_prompt, cache_sb_k, cache_sb_v, cache_mem_k, cache_mem_v, state_wkv, state_shift, page_table, w_in, mu_shift, w0_decay, w_decay2, a0, w_a2, w_g2, k_k, k_a, r_k, gn_g, gn_b, sb_bias, w_mem_kv, w_br_a, w_br_b, w_br_c, w_out, ln1_g, ln1_b, w_router, b_router, w1, w3, w2, ln2_g, ln2_b):
    depth = w_in.shape[0]
    n_p, t_p = x_prompt.shape[:2]
    n_s, t_s = x_sample.shape[:2]
    n_pool = cache_sb_k.shape[1]
    row = lambda a, l: a[l].reshape(1, -1)
    shared = {'wr_t': w_router.T, 'br': b_router.reshape(-1, 1)}

    xp = x_prompt.reshape(n_p * t_p, D_MODEL)
    xs = jnp.pad(x_sample, ((0, 0), (0, T_PAD - t_s), (0, 0))).reshape(n_s * T_PAD, D_MODEL)
    mem_rows = mem_prompt.reshape(n_p * N_MEM, D_MODEL)
    zero_wkv = jnp.zeros((n_p, N_HEAD, HEAD, HEAD), F32)
    zero_shift = jnp.zeros((n_p, N_SHIFT), F32)

    outs = [[] for _ in range(10)]
    for l in range(depth):
        lw = {'w_in': w_in[l].astype(BF16), 'mu_shift': row(mu_shift, l), 'w0_decay': row(w0_decay, l),
              'w_decay2': w_decay2[l].astype(BF16), 'a0': row(a0, l), 'w_a2': w_a2[l].astype(BF16),
              'w_g2': w_g2[l].astype(BF16), 'k_k': row(k_k, l), 'k_a': row(k_a, l), 'r_k': row(r_k, l),
              'gn_g': row(gn_g, l), 'gn_b': row(gn_b, l),
              'w_br_a': w_br_a[l].astype(BF16), 'w_br_b': w_br_b[l].astype(BF16), 'w_br_c': w_br_c[l].astype(BF16),
              'w_out': w_out[l].astype(BF16), 'ln1_g': row(ln1_g, l), 'ln1_b': row(ln1_b, l),
              'w1': w1[l].astype(BF16), 'w3': w3[l].astype(BF16), 'w2': w2[l].astype(BF16),
              'ln2_g': row(ln2_g, l), 'ln2_b': row(ln2_b, l)}
        bias = sb_bias[l]

        mem_kv = _matmul(mem_rows, w_mem_kv[l].astype(BF16))
        mk = mem_kv[:, :W_C].reshape(n_p, N_MEM, W_C)
        mv = mem_kv[:, W_C:].reshape(n_p, N_MEM, W_C)
        sb_p = functools.partial(_sb_prompt, bias=bias, n_seq=n_p, t_total=t_p)
        xp, kb, vb, wkv, sh = _trunk_layer(xp, lw, shared, lambda q16, k, k16, v, v16: sb_p(q16, k16, v16), mk, mv,
                                           zero_wkv, zero_shift, n_seq=n_p, t_total=t_p, t_valid=t_p)
        for i, a in zip((0, 1, 4, 5, 6, 8), (kb.reshape(n_p, t_p, N_HEAD, HEAD), vb.reshape(n_p, t_p, N_HEAD, HEAD),
                                             mk.reshape(n_p, N_MEM, H_C, HD_C), mv.reshape(n_p, N_MEM, H_C, HD_C), wkv, sh)):
            outs[i].append(a)

        def sb_s(q16, k, k16, v, v16):
            return _sb_sample(q16.astype(F32), k, v, bias, cache_sb_k, cache_sb_v, page_table, l)

        xs, kb, vb, wkv, sh = _trunk_layer(xs, lw, shared, sb_s, cache_mem_k[l].reshape(n_s, N_MEM, W_C),
                                           cache_mem_v[l].reshape(n_s, N_MEM, W_C), state_wkv[l], state_shift[l],
                                           n_seq=n_s, t_total=T_PAD, t_valid=t_s)
        for i, a in zip((2, 3, 7, 9), (kb.reshape(n_s, T_PAD, N_HEAD, HEAD)[:, :t_s], vb.reshape(n_s, T_PAD, N_HEAD, HEAD)[:, :t_s],
                                       wkv, sh)):
            outs[i].append(a)

    st = [jnp.stack(o) for o in outs]
    y_p = xp.reshape(n_p, t_p, D_MODEL)
    y_s = xs.reshape(n_s, T_PAD, D_MODEL)[:, :t_s]
    return (y_p, y_s, st[0], st[1], st[2], st[3], st[4], st[5], st[6], st[7], st[8], st[9])
```

```python
import functools

import numpy as np
import jax
import jax.numpy as jnp
from jax import lax
from jax.experimental import pallas as pl
from jax.experimental.pallas import tpu as pltpu

F32 = jnp.float32
BF16 = jnp.bfloat16

D_MODEL = 1024
DEPTH = 4
N_HEAD = 8
HEAD = 64
W_A = N_HEAD * HEAD
R_DECAY, R_AAA, R_GATE = 64, 64, 128
H_C, HD_C = 4, 128
W_C = H_C * HD_C
N_MEM = 256
N_SHIFT = 3 * W_A + R_DECAY + R_AAA + R_GATE
N_GATE = 3 * D_MODEL
N_IN = N_SHIFT + 3 * W_A + W_C + N_GATE
N_EXPERTS, N_GROUPS, E_PER_GROUP, D_FF = 16, 4, 4, 512
DN_ALPHA = (2 * DEPTH) ** 0.25
LN_EPS = 1e-5
GN_EPS = 64e-5
NORM_EPS = 1e-12
PAGE = 128
LANE = 128
SUBLANE = 8
T_PAD = SUBLANE
VMEM_LIMIT = 56 * 1024 * 1024


def _cparams(sem):
    return pltpu.CompilerParams(dimension_semantics=sem, vmem_limit_bytes=VMEM_LIMIT)


def _softplus(z):
    return jnp.maximum(z, 0.0) + jnp.log1p(jnp.exp(-jnp.abs(z)))


def _sigmoid(z):
    return 1.0 / (1.0 + jnp.exp(-z))


def _dot(a, b):
    return jnp.dot(a, b, preferred_element_type=F32)


def _dot_nt(a, b):
    return lax.dot_general(a, b, (((1,), (1,)), ((), ())), preferred_element_type=F32)


def _bf16_pieces(a, n):
    pieces = []
    for _ in range(n):
        p = a.astype(BF16)
        pieces.append(p)
        a = a - p.astype(F32)
    return pieces


def _dot_f32(a, b, ca=1, cb=0):
    ah, al = _bf16_pieces(a, 2)
    bh, bl = _bf16_pieces(b, 2)
    return lax.dot_general(jnp.concatenate([ah, ah, al], axis=ca), jnp.concatenate([bh, bl, bh], axis=cb),
                           (((ca,), (cb,)), ((), ())), preferred_element_type=F32)


def _dot_bf16(a, b, ca=1, cb=0):
    return lax.dot_general(a.astype(BF16), b.astype(BF16), (((ca,), (cb,)), ((), ())), preferred_element_type=F32)


def _dot_mask_f32(mask, b):
    m = mask.astype(BF16)
    return _dot(jnp.concatenate([m, m, m], axis=1), jnp.concatenate(_bf16_pieces(b, 3), axis=0))


def _layer_norm(y, g, b):
    mu = jnp.mean(y, axis=-1, keepdims=True)
    yc = y - mu
    var = jnp.mean(yc * yc, axis=-1, keepdims=True)
    return yc * lax.rsqrt(var + LN_EPS) * g + b


_MM_COLS = 256
_SB_SCALE = HEAD ** -0.5
_IN_GROUPS = ((N_SHIFT, True, False, 1.0),
              (W_A, False, True, _SB_SCALE),
              (W_A, True, True, 1.0),
              (W_A, True, True, 1.0),
              (W_C, True, False, 1.0),
              (N_GATE, True, False, 1.0))


_IN_KV_GROUPS = (2, 3)


def _in_proj_body(x_ref, w_ref, *out_refs, kv_transposed):
    x = x_ref[...].astype(BF16)
    refs = iter(out_refs)
    off = 0
    for gi, (width, want_f32, want_bf16, scale) in enumerate(_IN_GROUPS):
        ref_f32 = next(refs) if want_f32 else None
        ref_bf16 = next(refs) if want_bf16 else None
        for c in range(0, width, _MM_COLS):
            cols = slice(c, c + _MM_COLS)
            y = _dot(x, w_ref[:, off + c:off + c + _MM_COLS])
            if want_f32 and kv_transposed and gi in _IN_KV_GROUPS:
                ref_f32[cols, :] = y.T
            elif want_f32:
                ref_f32[:, cols] = y
            if want_bf16:
                ref_bf16[:, cols] = (y * scale).astype(BF16)
        off += width


def _in_proj(x, w_in_bf16, kv_seq=None):
    n = x.shape[0]
    bm = min(n, 256)
    row_spec = lambda w: pl.BlockSpec((bm, w), lambda i: (i, 0))
    specs, shapes = [], []
    for gi, (width, want_f32, want_bf16, _) in enumerate(_IN_GROUPS):
        if want_f32 and kv_seq is not None and gi in _IN_KV_GROUPS:
            n_seq, t_total = kv_seq
            nt = t_total // bm
            specs.append(pl.BlockSpec((width, bm), lambda i: (i // nt, i % nt)))
            shapes.append(jax.ShapeDtypeStruct((n_seq * width, t_total), F32))
        elif want_f32:
            specs.append(row_spec(width))
            shapes.append(jax.ShapeDtypeStruct((n, width), F32))
        if want_bf16:
            specs.append(row_spec(width))
            shapes.append(jax.ShapeDtypeStruct((n, width), BF16))
    return pl.pallas_call(
        functools.partial(_in_proj_body, kv_transposed=kv_seq is not None),
        grid=(n // bm,),
        in_specs=[pl.BlockSpec((bm, D_MODEL), lambda i: (i, 0)),
                  pl.BlockSpec((D_MODEL, N_IN), lambda i: (0, 0), pipeline_mode=pl.Buffered(1))],
        out_specs=specs,
        out_shape=shapes,
        compiler_params=_cparams(("parallel",)),
        name="in_proj",
    )(x, w_in_bf16)


def _matmul_body(x_ref, w_ref, o_ref):
    o_ref[...] = _dot(x_ref[...].astype(BF16), w_ref[...])


def _matmul(x, w_bf16, bn=512):
    n, k = x.shape
    m = w_bf16.shape[1]
    bm = min(n, 256)
    return pl.pallas_call(
        _matmul_body,
        grid=(n // bm, m // bn),
        in_specs=[pl.BlockSpec((bm, k), lambda i, j: (i, 0)), pl.BlockSpec((k, bn), lambda i, j: (0, j))],
        out_specs=pl.BlockSpec((bm, bn), lambda i, j: (i, j)),
        out_shape=jax.ShapeDtypeStruct((n, m), F32),
        compiler_params=_cparams(("parallel", "parallel")),
        name="matmul",
    )(x, w_bf16)


def _head_sum(x):
    n = x.shape[-1]
    same = (lax.broadcasted_iota(jnp.int32, (n, n), 0) // HEAD) == (lax.broadcasted_iota(jnp.int32, (n, n), 1) // HEAD)
    same = same.astype(BF16)
    return _dot(jnp.concatenate(_bf16_pieces(x, 3), axis=1), jnp.concatenate([same, same, same], axis=0))


def _rwkv_prep_body(p_ref, sh0_ref, mu_ref, w0_ref, wd2_ref, a0_ref, wa2_ref, wg2_ref, kk_ref, ka_ref, rk_ref,
                    r_o, lw_o, k_o, v_o, kn_o, bb_o, g_o, bonus_o, shift_o, carry, *, tb, t_valid, t_total):
    t = pl.program_id(1)

    @pl.when(t == 0)
    def _():
        carry[...] = sh0_ref[0]

    p = p_ref[...]
    row = lax.broadcasted_iota(jnp.int32, (tb, 1), 0)
    prev = jnp.where(row == 0, carry[...], pltpu.roll(p, 1, 0))
    carry[...] = p[tb - 1:tb, :]

    last = t_valid - 1
    @pl.when(t == last // tb)
    def _():
        shift_o[0] = p[last % tb:last % tb + 1, :]

    z = p + mu_ref[...] * (prev - p)
    r = z[:, 0:W_A]
    k = z[:, W_A:2 * W_A]
    v = z[:, 2 * W_A:3 * W_A]
    c0 = 3 * W_A
    wl = z[:, c0:c0 + R_DECAY]
    al = z[:, c0 + R_DECAY:c0 + R_DECAY + R_AAA]
    gl = z[:, c0 + R_DECAY + R_AAA:N_SHIFT]

    w_log = -_softplus(-(w0_ref[...] + _dot(jnp.tanh(wl).astype(BF16), wd2_ref[...]))) - 0.5
    lw = -jnp.exp(w_log)
    a = _sigmoid(a0_ref[...] + _dot(al.astype(BF16), wa2_ref[...]))
    g = _dot(_sigmoid(gl).astype(BF16), wg2_ref[...])
    kn = k * kk_ref[...]
    kn = kn / jnp.maximum(jnp.sqrt(_head_sum(kn * kn)), NORM_EPS)
    k2 = k * (1.0 + (a - 1.0) * ka_ref[...])
    bonus = _head_sum(r * k2 * rk_ref[...]) * v
    bb = kn * a
    if t_valid < t_total:
        ok = (t * tb + row) < t_valid
        lw = jnp.where(ok, lw, 0.0)
        kn = jnp.where(ok, kn, 0.0)
        bb = jnp.where(ok, bb, 0.0)
        k2 = jnp.where(ok, k2, 0.0)
        v = jnp.where(ok, v, 0.0)
    r_o[...] = r
    lw_o[...] = lw
    k_o[...] = k2
    v_o[...] = v
    kn_o[...] = kn
    bb_o[...] = bb
    g_o[...] = g
    bonus_o[...] = bonus


def _rwkv_prep(p_shift, shift0, lw, *, n_seq, t_total, t_valid):
    tb = min(t_total, 256)
    nt = t_total // tb
    n = n_seq * t_total
    row_spec = lambda w: pl.BlockSpec((tb, w), lambda b, t: (b * nt + t, 0))
    par = lambda a: pl.BlockSpec(a.shape, lambda b, t: (0,) * a.ndim)
    params = (lw['mu_shift'], lw['w0_decay'], lw['w_decay2'], lw['a0'], lw['w_a2'], lw['w_g2'],
              lw['k_k'], lw['k_a'], lw['r_k'])
    outs = pl.pallas_call(
        functools.partial(_rwkv_prep_body, tb=tb, t_valid=t_valid, t_total=t_total),
        grid=(n_seq, nt),
        in_specs=[row_spec(N_SHIFT), pl.BlockSpec((1, 1, N_SHIFT), lambda b, t: (b, 0, 0))] + [par(a) for a in params],
        out_specs=[row_spec(W_A)] * 8 + [pl.BlockSpec((1, 1, N_SHIFT), lambda b, t: (b, 0, 0))],
        out_shape=[jax.ShapeDtypeStruct((n, W_A), F32)] * 8 + [jax.ShapeDtypeStruct((n_seq, 1, N_SHIFT), F32)],
        scratch_shapes=[pltpu.VMEM((1, N_SHIFT), F32)],
        compiler_params=_cparams(("parallel", "arbitrary")),
        name="rwkv_prep",
    )(p_shift, shift0.reshape(n_seq, 1, N_SHIFT), *params)
    return outs[:8], outs[8].reshape(n_seq, N_SHIFT)


_INV_BLOCK = 16


def _nilpotent_inverses(ns, steps):
    invs = [-n for n in ns]
    sqs = list(ns)
    for _ in range(steps - 1):
        sqs = [_dot_f32(q, q) for q in sqs]
        invs = [i + q + _dot_f32(i, q) for i, q in zip(invs, sqs)]
    return invs


def _unit_lower_inverses_minus_i(ns, c):
    blk = min(_INV_BLOCK, c)
    if blk == c:
        return _nilpotent_inverses(ns, int(np.log2(c)))
    ri = lax.broadcasted_iota(jnp.int32, (c, c), 0)
    ci = lax.broadcasted_iota(jnp.int32, (c, c), 1)
    on_diag = (ri // blk) == (ci // blk)
    eye = (ri == ci).astype(F32)
    d_invs = [d + eye for d in _nilpotent_inverses([jnp.where(on_diag, n, 0.0) for n in ns], int(np.log2(blk)))]
    ms = [_dot_f32(d, jnp.where(on_diag, 0.0, n)) for d, n in zip(d_invs, ns)]
    m_invs = [m + eye for m in _nilpotent_inverses(ms, int(np.log2(c // blk)))]
    return [_dot_f32(m, d) - eye for m, d in zip(m_invs, d_invs)]


def _wkv_body(r_ref, lw_ref, k_ref, v_ref, kn_ref, bb_ref, g_ref, bonus_ref, s0_ref, gng_ref, gnb_ref,
              o_ref, sT_ref, state, o_scr, *, c):
    ci = pl.program_id(1)

    @pl.when(ci == 0)
    def _():
        state[...] = s0_ref[0]

    ri = lax.broadcasted_iota(jnp.int32, (c, c), 0)
    cj = lax.broadcasted_iota(jnp.int32, (c, c), 1)
    incl = ri >= cj
    strict = ri > cj

    lw = lw_ref[...]
    cs = _dot_mask_f32(incl, lw)
    tot = cs[c - 1:c, :]
    e_out = jnp.exp(-cs)
    e_end = jnp.exp(tot - cs)
    k = k_ref[...]
    bb = bb_ref[...]
    r_t = r_ref[...] * jnp.exp(cs)
    kap = kn_ref[...] * jnp.exp(cs - lw)
    k_t = k * e_out
    b_t = bb * e_out
    k_d = k * e_end
    b_d = bb * e_end
    e_tot = jnp.exp(tot)
    v = v_ref[...]

    heads = [slice(h * HEAD, (h + 1) * HEAD) for h in range(N_HEAD)]
    kap_h = [kap[:, sl] for sl in heads]
    v_h = [v[:, sl] for sl in heads]
    kr = [jnp.concatenate([kp, r_t[:, sl]], axis=0) for kp, sl in zip(kap_h, heads)]
    g_k = [_dot_f32(x, k_t[:, sl], 1, 1) for x, sl in zip(kr, heads)]
    g_b = [_dot_f32(x, b_t[:, sl], 1, 1) for x, sl in zip(kr, heads)]
    a_kk = [jnp.where(strict, g[:c], 0.0) for g in g_k]
    a_kr = [jnp.where(incl, g[c:], 0.0) for g in g_k]
    a_bk = [jnp.where(strict, g[:c], 0.0) for g in g_b]
    a_br = [jnp.where(incl, g[c:], 0.0) for g in g_b]
    t_inv = _unit_lower_inverses_minus_i(a_bk, c)
    rhs0 = [_dot_bf16(a, x) for a, x in zip(a_kk, v_h)]
    u0 = [x + _dot_f32(t, x) for t, x in zip(t_inv, rhs0)]
    w1 = [x + _dot_f32(t, x) for t, x in zip(t_inv, kap_h)]
    s_h = [state[h] for h in range(N_HEAD)]
    su = [_dot_f32(jnp.concatenate([w, r_t[:, sl]], axis=0), s, 1, 1) for w, sl, s in zip(w1, heads, s_h)]
    u = [x[:c] + y for x, y in zip(su, u0)]
    o = [x[c:] + _dot_bf16(a, y) - _dot_bf16(b, z) for x, a, y, b, z in zip(su, a_kr, v_h, a_br, u)]
    for h, sl in enumerate(heads):
        vu = jnp.concatenate([v_h[h], -u[h]], axis=0)
        kb = jnp.concatenate([k_d[:, sl], b_d[:, sl]], axis=0)
        state[h] = s_h[h] * e_tot[:, sl] + _dot_bf16(vu, kb, 0, 0)
    for h, sl in enumerate(heads):
        mu = jnp.mean(o[h], axis=-1, keepdims=True)
        oc = o[h] - mu
        var = jnp.mean(oc * oc, axis=-1, keepdims=True)
        o_scr[:, sl] = oc * lax.rsqrt(var + GN_EPS)

    o_ref[...] = (o_scr[...] * gng_ref[...] + gnb_ref[...] + bonus_ref[...]) * g_ref[...]

    @pl.when(ci == pl.num_programs(1) - 1)
    def _():
        sT_ref[0] = state[...]


def _wkv(prep, state0, gn_g, gn_b, *, n_seq, t_total):
    c = min(t_total, 64)
    nc = t_total // c
    n = n_seq * t_total
    row = pl.BlockSpec((c, W_A), lambda b, i: (b * nc + i, 0))
    st = pl.BlockSpec((1, N_HEAD, HEAD, HEAD), lambda b, i: (b, 0, 0, 0))
    par = pl.BlockSpec((1, W_A), lambda b, i: (0, 0))
    return pl.pallas_call(
        functools.partial(_wkv_body, c=c),
        grid=(n_seq, nc),
        in_specs=[row] * 8 + [st, par, par],
        out_specs=[row, st],
        out_shape=[jax.ShapeDtypeStruct((n, W_A), F32), jax.ShapeDtypeStruct((n_seq, N_HEAD, HEAD, HEAD), F32)],
        scratch_shapes=[pltpu.VMEM((N_HEAD, HEAD, HEAD), F32), pltpu.VMEM((c, W_A), F32)],
        compiler_params=_cparams(("parallel", "arbitrary")),
        name="wkv",
    )(*prep, state0, gn_g, gn_b)


def _later_key_mask(n):
    j = lax.broadcasted_iota(jnp.int32, (n, n), 0)
    s = lax.broadcasted_iota(jnp.int32, (n, n), 1)
    return jnp.where(j > s, -1.0, 0.0).astype(BF16)


_SP_LINEAR = 20.0


def _sb_scores(z, carry, later, visible=None):
    sp = jnp.where(z > _SP_LINEAR, z, jnp.log(1.0 + jnp.exp(jnp.minimum(z, _SP_LINEAR))))
    if visible is not None:
        sp = jnp.where(visible, sp, 0.0)
    tail = _dot(sp.astype(BF16), later)
    att = jnp.exp(z - sp + tail + carry)
    if visible is not None:
        att = jnp.where(visible, att, 0.0)
    return att, carry + (tail - sp)[:, 0:1]


_SB_TQ = 512
_SB_RQ = 512
_SB_KW = 256


def _sb_prompt_body(qi_ref, kj_ref, q_ref, k_ref, v_ref, bias_ref, o_ref, acc, carry, *, tq, rq, kw):
    s = pl.program_id(1)
    qi = qi_ref[s]
    kj = kj_ref[s]
    later = _later_key_mask(kw)

    @pl.when(kj == qi)
    def _():
        acc[...] = jnp.zeros_like(acc)
        carry[...] = jnp.zeros_like(carry)

    def sweep(diagonal):
        masks = {}

        def visible_mask(qs, ks):
            if (qs, ks) not in masks:
                row_id = qs * rq + lax.broadcasted_iota(jnp.int32, (rq, kw), 0)
                col_id = ks * kw + lax.broadcasted_iota(jnp.int32, (rq, kw), 1)
                masks[qs, ks] = col_id < row_id
            return masks[qs, ks]

        for h in range(N_HEAD):
            sl = slice(h * HEAD, (h + 1) * HEAD)
            bias = bias_ref[h]
            for qs in range(tq // rq):
                rows = slice(qs * rq, (qs + 1) * rq)
                q = q_ref[rows, sl]
                cr = carry[h, rows, :]
                a = acc[rows, sl]
                for ks in reversed(range(tq // kw)):
                    visible = None
                    if diagonal:
                        if ks * kw >= (qs + 1) * rq:
                            continue
                        if (ks + 1) * kw > qs * rq:
                            visible = visible_mask(qs, ks)
                    keys = slice(ks * kw, (ks + 1) * kw)
                    att, cr = _sb_scores(_dot_nt(q, k_ref[keys, sl]) + bias, cr, later, visible)
                    a = a + _dot(att.astype(BF16), v_ref[keys, sl])
                acc[rows, sl] = a
                carry[h, rows, :] = cr

    @pl.when(kj == qi)
    def _():
        sweep(True)

    @pl.when(kj != qi)
    def _():
        sweep(False)

    @pl.when(kj == 0)
    def _():
        o_ref[...] = acc[...]


def _sb_prompt(q, k, v, bias, *, n_seq, t_total):
    tq = min(_SB_TQ, t_total)
    rq = min(_SB_RQ, tq)
    kw = min(_SB_KW, tq)
    nq = t_total // tq
    pairs = [(i, j) for i in range(nq) for j in range(i, -1, -1)]
    qi = jnp.asarray([p[0] for p in pairs], jnp.int32)
    kj = jnp.asarray([p[1] for p in pairs], jnp.int32)
    n = n_seq * t_total
    return pl.pallas_call(
        functools.partial(_sb_prompt_body, tq=tq, rq=rq, kw=kw),
        grid_spec=pltpu.PrefetchScalarGridSpec(
            num_scalar_prefetch=2,
            grid=(n_seq, len(pairs)),
            in_specs=[pl.BlockSpec((tq, W_A), lambda b, s, qi, kj: (b * nq + qi[s], 0)),
                      pl.BlockSpec((tq, W_A), lambda b, s, qi, kj: (b * nq + kj[s], 0)),
                      pl.BlockSpec((tq, W_A), lambda b, s, qi, kj: (b * nq + kj[s], 0)),
                      pl.BlockSpec(memory_space=pltpu.SMEM)],
            out_specs=pl.BlockSpec((tq, W_A), lambda b, s, qi, kj: (b * nq + qi[s], 0)),
            scratch_shapes=[pltpu.VMEM((tq, W_A), F32), pltpu.VMEM((N_HEAD, tq, 1), F32)]),
        out_shape=jax.ShapeDtypeStruct((n, W_A), F32),
        compiler_params=_cparams(("parallel", "arbitrary")),
        name="sb_prompt",
    )(qi, kj, q, k, v, bias)


_SB_PAGES = 8


def _sb_sample_body(pt_ref, q_ref, kn_ref, vn_ref, bias_ref, *refs):
    k_refs = refs[:_SB_PAGES]
    v_refs = refs[_SB_PAGES:2 * _SB_PAGES]
    o_ref, bias_col, acc, carry = refs[2 * _SB_PAGES:]
    j = pl.program_id(1)
    m = N_HEAD * T_PAD
    later = _later_key_mask(PAGE)

    def page(k_ref, v_ref, visible=None):
        z = jnp.concatenate([_dot(q_ref[0, h].astype(BF16), k_ref[0, h].astype(BF16)) for h in range(N_HEAD)], axis=0)
        att, cr = _sb_scores(z + bias_col[...], carry[...], later, visible)
        carry[...] = cr
        for h in range(N_HEAD):
            acc[h] += _dot_nt(att[h * T_PAD:(h + 1) * T_PAD, :].astype(BF16), v_ref[0, h].astype(BF16))

    @pl.when(j == 0)
    def _():
        row_head = lax.broadcasted_iota(jnp.int32, (m, 1), 0) // T_PAD
        b = jnp.zeros((m, 1), F32)
        for h in range(N_HEAD):
            b = jnp.where(row_head == h, bias_ref[h], b)
        bias_col[...] = b
        acc[...] = jnp.zeros_like(acc)
        carry[...] = jnp.zeros_like(carry)
        tok = lax.broadcasted_iota(jnp.int32, (m, PAGE), 0) % T_PAD
        key = lax.broadcasted_iota(jnp.int32, (m, PAGE), 1)
        page(kn_ref, vn_ref, key < tok)

    for r in range(_SB_PAGES):
        page(k_refs[r], v_refs[r])

    @pl.when(j == pl.num_programs(1) - 1)
    def _():
        for h in range(N_HEAD):
            o_ref[0, :, h * HEAD:(h + 1) * HEAD] = acc[h]


def _sb_sample(q, k_new, v_new, bias, k_cache, v_cache, page_table, layer):
    n_seq, n_pages = page_table.shape
    depth, n_pool = k_cache.shape[:2]
    as_pages = lambda c: c.transpose(0, 1, 3, 4, 2).reshape(depth * n_pool, N_HEAD, HEAD, PAGE)
    q4 = q.reshape(n_seq, T_PAD, N_HEAD, HEAD).transpose(0, 2, 1, 3)

    def new_page(x):
        x = x.reshape(n_seq, T_PAD, N_HEAD, HEAD).transpose(0, 2, 3, 1)
        return jnp.pad(x, ((0, 0), (0, 0), (0, 0), (0, PAGE - T_PAD)))

    page_block = (1, N_HEAD, HEAD, PAGE)
    seq_page = pl.BlockSpec(page_block, lambda b, j, pt: (b, 0, 0, 0))

    def page_spec(r):
        return pl.BlockSpec(page_block,
                            lambda b, j, pt: (layer * n_pool + pt[b * n_pages + n_pages - 1 - (j * _SB_PAGES + r)], 0, 0, 0))

    m = N_HEAD * T_PAD
    out = pl.pallas_call(
        _sb_sample_body,
        grid_spec=pltpu.PrefetchScalarGridSpec(
            num_scalar_prefetch=1,
            grid=(n_seq, n_pages // _SB_PAGES),
            in_specs=[pl.BlockSpec((1, N_HEAD, T_PAD, HEAD), lambda b, j, pt: (b, 0, 0, 0)), seq_page, seq_page,
                      pl.BlockSpec(memory_space=pltpu.SMEM)] + [page_spec(r) for r in range(_SB_PAGES)] * 2,
            out_specs=pl.BlockSpec((1, T_PAD, W_A), lambda b, j, pt: (b, 0, 0)),
            scratch_shapes=[pltpu.VMEM((m, 1), F32), pltpu.VMEM((N_HEAD, T_PAD, HEAD), F32), pltpu.VMEM((m, 1), F32)]),
        out_shape=jax.ShapeDtypeStruct((n_seq, T_PAD, W_A), F32),
        compiler_params=_cparams(("parallel", "arbitrary")),
        name="sb_sample",
    )(page_table.reshape(-1), q4, new_page(k_new), new_page(v_new), bias,
      *([as_pages(k_cache)] * _SB_PAGES), *([as_pages(v_cache)] * _SB_PAGES))
    return out.reshape(n_seq * T_PAD, W_A)


def _mem_attn_body(q_ref, k_ref, v_ref, o_ref):
    for h in range(H_C):
        sl = slice(h * HD_C, (h + 1) * HD_C)
        s = _dot_nt(q_ref[:, sl].astype(BF16), k_ref[0, :, sl].astype(BF16)) * (HD_C ** -0.5)
        e = jnp.exp(s - jnp.max(s, axis=-1, keepdims=True))
        prob = e / jnp.sum(e, axis=-1, keepdims=True)
        o_ref[:, sl] = _dot(prob.astype(BF16), v_ref[0, :, sl].astype(BF16))


def _mem_attn(q, mem_k, mem_v, *, n_seq, t_total):
    tq = min(t_total, 512)
    nq = t_total // tq
    mem = pl.BlockSpec((1, N_MEM, W_C), lambda b, i: (b, 0, 0))
    row = pl.BlockSpec((tq, W_C), lambda b, i: (b * nq + i, 0))
    return pl.pallas_call(
        _mem_attn_body,
        grid=(n_seq, nq),
        in_specs=[row, mem, mem],
        out_specs=row,
        out_shape=jax.ShapeDtypeStruct((n_seq * t_total, W_C), F32),
        compiler_params=_cparams(("parallel", "parallel")),
        name="mem_attn",
    )(q, mem_k, mem_v)


def _combine_body(x_ref, oa_ref, ob_ref, oc_ref, gate_ref, wa_ref, wb_ref, wc_ref, wo_ref, g_ref, b_ref, y_ref):
    h = None
    for i, (o_ref, w_ref) in enumerate(((oa_ref, wa_ref), (ob_ref, wb_ref), (oc_ref, wc_ref))):
        part = _sigmoid(gate_ref[:, i * D_MODEL:(i + 1) * D_MODEL]) * _dot(o_ref[...].astype(BF16), w_ref[...])
        h = part if h is None else h + part
    y = DN_ALPHA * x_ref[...] + _dot(h.astype(BF16), wo_ref[...])
    y_ref[...] = _layer_norm(y, g_ref[...], b_ref[...])


def _combine(x, o_a, o_b, o_c, gate, wa, wb, wc, wo, ln_g, ln_b):
    n = x.shape[0]
    bm = min(n, 256)
    row = lambda w: pl.BlockSpec((bm, w), lambda i: (i, 0))
    const = lambda a: pl.BlockSpec(a.shape, lambda i: (0,) * a.ndim, pipeline_mode=pl.Buffered(1))
    return pl.pallas_call(
        _combine_body,
        grid=(n // bm,),
        in_specs=[row(D_MODEL), row(W_A), row(W_A), row(W_C), row(N_GATE)] + [const(a) for a in (wa, wb, wc, wo, ln_g, ln_b)],
        out_specs=row(D_MODEL),
        out_shape=jax.ShapeDtypeStruct((n, D_MODEL), F32),
        compiler_params=_cparams(("parallel",)),
        name="combine",
    )(x, o_a, o_b, o_c, gate, wa, wb, wc, wo, ln_g, ln_b)


def _top2(scores, ids, lowest):
    m1 = jnp.max(scores, axis=0, keepdims=True)
    i1 = jnp.min(jnp.where(scores == m1, ids, N_EXPERTS), axis=0, keepdims=True)
    rest = jnp.where(ids == i1, lowest, scores)
    m2 = jnp.max(rest, axis=0, keepdims=True)
    i2 = jnp.min(jnp.where(rest == m2, ids, N_EXPERTS), axis=0, keepdims=True)
    return m1, i1, m2, i2


def _router_gates(x, wr_t, br):
    m = x.shape[0]
    logits = _dot_f32(wr_t, x, 1, 1) + br
    e = jnp.exp(logits - jnp.max(logits, axis=0, keepdims=True))
    scores = e / jnp.sum(e, axis=0, keepdims=True)
    ids = lax.broadcasted_iota(jnp.int32, (N_EXPERTS, m), 0)
    grp = ids // E_PER_GROUP
    best = None
    for g in range(N_GROUPS):
        m1, _, m2, _ = _top2(jnp.where(grp == g, scores, -1.0), ids, -2.0)
        gs = m1 + m2
        if best is None:
            best, sel = gs, jnp.zeros((1, m), jnp.int32)
        else:
            sel = jnp.where(gs > best, g, sel)
            best = jnp.maximum(gs, best)
    m1, i1, m2, i2 = _top2(jnp.where(grp == sel, scores, -1.0), ids, -2.0)
    tot = m1 + m2
    return jnp.where(ids == i1, m1 / tot, 0.0) + jnp.where(ids == i2, m2 / tot, 0.0)


def _moe_body(x_ref, wr_ref, br_ref, w1_ref, w3_ref, w2_ref, g_ref, b_ref, y_ref, gate_scr, acc):
    e = pl.program_id(1)
    x = x_ref[...]

    @pl.when(e == 0)
    def _():
        gate_scr[...] = _router_gates(x, wr_ref[...], br_ref[...]).T
        acc[...] = jnp.zeros_like(acc)

    lane = lax.broadcasted_iota(jnp.int32, gate_scr.shape, 1)
    ge = jnp.sum(jnp.where(lane == e, gate_scr[...], 0.0), axis=1, keepdims=True)
    xb = x.astype(BF16)
    u = _dot(xb, w1_ref[0])
    h = (u * _sigmoid(u)) * _dot(xb, w3_ref[0])
    acc[...] += _dot((h * ge).astype(BF16), w2_ref[0])

    @pl.when(e == pl.num_programs(1) - 1)
    def _():
        y_ref[...] = _layer_norm(DN_ALPHA * x + acc[...], g_ref[...], b_ref[...])


def _moe(x, wr_t, br, w1, w3, w2, ln_g, ln_b):
    n = x.shape[0]
    bm = min(n, 1024)
    const = lambda a: pl.BlockSpec(a.shape, lambda i, e: (0,) * a.ndim)
    return pl.pallas_call(
        _moe_body,
        grid=(n // bm, N_EXPERTS),
        in_specs=[pl.BlockSpec((bm, D_MODEL), lambda i, e: (i, 0)), const(wr_t), const(br),
                  pl.BlockSpec((1, D_MODEL, D_FF), lambda i, e: (e, 0, 0)),
                  pl.BlockSpec((1, D_MODEL, D_FF), lambda i, e: (e, 0, 0)),
                  pl.BlockSpec((1, D_FF, D_MODEL), lambda i, e: (e, 0, 0)),
                  const(ln_g), const(ln_b)],
        out_specs=pl.BlockSpec((bm, D_MODEL), lambda i, e: (i, 0)),
        out_shape=jax.ShapeDtypeStruct((n, D_MODEL), F32),
        scratch_shapes=[pltpu.VMEM((bm, N_EXPERTS), F32), pltpu.VMEM((bm, D_MODEL), F32)],
        compiler_params=_cparams(("parallel", "arbitrary")),
        name="moe",
    )(x, wr_t, br, w1, w3, w2, ln_g, ln_b)


def _trunk_layer(x, lw, shared, sb_fn, mem_k, mem_v, wkv0, shift0, *, n_seq, t_total, t_valid, kv_channel_major):
    p_shift, q16, k_b, k16, v_b, v16, q_c, gate = _in_proj(x, lw['w_in'], (n_seq, t_total) if kv_channel_major else None)
    prep, shift = _rwkv_prep(p_shift, shift0, lw, n_seq=n_seq, t_total=t_total, t_valid=t_valid)
    o_a, wkv = _wkv(prep, wkv0, lw['gn_g'], lw['gn_b'], n_seq=n_seq, t_total=t_total)
    o_b = sb_fn(q16, k_b, k16, v_b, v16)
    o_c = _mem_attn(q_c, mem_k, mem_v, n_seq=n_seq, t_total=t_total)
    x = _combine(x, o_a, o_b, o_c, gate, lw['w_br_a'], lw['w_br_b'], lw['w_br_c'], lw['w_out'], lw['ln1_g'], lw['ln1_b'])
    x = _moe(x, shared['wr_t'], shared['br'], lw['w1'], lw['w3'], lw['w2'], lw['ln2_g'], lw['ln2_b'])
    return x, k_b, v_b, wkv, shift


def kernel(x_prompt, x_sample, mem_prompt, cache_sb_k, cache_sb_v, cache_mem_k, cache_mem_v, state_wkv, state_shift, page_table, w_in, mu_shift, w0_decay, w_decay2, a0, w_a2, w_g2, k_k, k_a, r_k, gn_g, gn_b, sb_bias, w_mem_kv, w_br_a, w_br_b, w_br_c, w_out, ln1_g, ln1_b, w_router, b_router, w1, w3, w2, ln2_g, ln2_b):
    depth = w_in.shape[0]
    n_p, t_p = x_prompt.shape[:2]
    n_s, t_s = x_sample.shape[:2]
    n_pool = cache_sb_k.shape[1]
    row = lambda a, l: a[l].reshape(1, -1)
    shared = {'wr_t': w_router.T, 'br': b_router.reshape(-1, 1)}

    xp = x_prompt.reshape(n_p * t_p, D_MODEL)
    xs = jnp.pad(x_sample, ((0, 0), (0, T_PAD - t_s), (0, 0))).reshape(n_s * T_PAD, D_MODEL)
    mem_rows = mem_prompt.reshape(n_p * N_MEM, D_MODEL)
    zero_wkv = jnp.zeros((n_p, N_HEAD, HEAD, HEAD), F32)
    zero_shift = jnp.zeros((n_p, N_SHIFT), F32)

    outs = [[] for _ in range(10)]
    for l in range(depth):
        lw = {'w_in': w_in[l].astype(BF16), 'mu_shift': row(mu_shift, l), 'w0_decay': row(w0_decay, l),
              'w_decay2': w_decay2[l].astype(BF16), 'a0': row(a0, l), 'w_a2': w_a2[l].astype(BF16),
              'w_g2': w_g2[l].astype(BF16), 'k_k': row(k_k, l), 'k_a': row(k_a, l), 'r_k': row(r_k, l),
              'gn_g': row(gn_g, l), 'gn_b': row(gn_b, l),
              'w_br_a': w_br_a[l].astype(BF16), 'w_br_b': w_br_b[l].astype(BF16), 'w_br_c': w_br_c[l].astype(BF16),
              'w_out': w_out[l].astype(BF16), 'ln1_g': row(ln1_g, l), 'ln1_b': row(ln1_b, l),
              'w1': w1[l].astype(BF16), 'w3': w3[l].astype(BF16), 'w2': w2[l].astype(BF16),
              'ln2_g': row(ln2_g, l), 'ln2_b': row(ln2_b, l)}
        bias = sb_bias[l]

        mem_kv = _matmul(mem_rows, w_mem_kv[l].astype(BF16))
        mk = mem_kv[:, :W_C].reshape(n_p, N_MEM, W_C)
        mv = mem_kv[:, W_C:].reshape(n_p, N_MEM, W_C)
        sb_p = functools.partial(_sb_prompt, bias=bias, n_seq=n_p, t_total=t_p)
        xp, kb, vb, wkv, sh = _trunk_layer(xp, lw, shared, lambda q16, k, k16, v, v16: sb_p(q16, k16, v16), mk, mv,
                                           zero_wkv, zero_shift, n_seq=n_p, t_total=t_p, t_valid=t_p, kv_channel_major=True)
        for i, a in zip((0, 1, 4, 5, 6, 8), (kb.reshape(n_p, N_HEAD, HEAD, t_p), vb.reshape(n_p, N_HEAD, HEAD, t_p),
                                             mk.reshape(n_p, N_MEM, H_C, HD_C), mv.reshape(n_p, N_MEM, H_C, HD_C), wkv, sh)):
            outs[i].append(a)

        def sb_s(q16, k, k16, v, v16):
            return _sb_sample(q16.astype(F32), k, v, bias, cache_sb_k, cache_sb_v, page_table, l)

        xs, kb, vb, wkv, sh = _trunk_layer(xs, lw, shared, sb_s, cache_mem_k[l].reshape(n_s, N_MEM, W_C),
                                           cache_mem_v[l].reshape(n_s, N_MEM, W_C), state_wkv[l], state_shift[l],
                                           n_seq=n_s, t_total=T_PAD, t_valid=t_s, kv_channel_major=False)
        for i, a in zip((2, 3, 7, 9), (kb.reshape(n_s, T_PAD, N_HEAD, HEAD)[:, :t_s], vb.reshape(n_s, T_PAD, N_HEAD, HEAD)[:, :t_s],
                                       wkv, sh)):
            outs[i].append(a)

    st = [jnp.stack(o) for o in outs]
    for i in (0, 1):
        st[i] = st[i].transpose(0, 1, 4, 2, 3)
    y_p = xp.reshape(n_p, t_p, D_MODEL)
    y_s = xs.reshape(n_s, T_PAD, D_MODEL)[:, :t_s]
    return (y_p, y_s, st[0], st[1], st[2], st[3], st[4], st[5], st[6], st[7], st[8], st[9])
```

```python
import functools

import numpy as np
import jax
import jax.numpy as jnp
from jax import lax
from jax.experimental import pallas as pl
from jax.experimental.pallas import tpu as pltpu

F32 = jnp.float32
BF16 = jnp.bfloat16

D_MODEL = 1024
DEPTH = 4
N_HEAD = 8
HEAD = 64
W_A = N_HEAD * HEAD
R_DECAY, R_AAA, R_GATE = 64, 64, 128
H_C, HD_C = 4, 128
W_C = H_C * HD_C
N_MEM = 256
N_SHIFT = 3 * W_A + R_DECAY + R_AAA + R_GATE
N_GATE = 3 * D_MODEL
N_IN = N_SHIFT + 3 * W_A + W_C + N_GATE
N_EXPERTS, N_GROUPS, E_PER_GROUP, D_FF = 16, 4, 4, 512
DN_ALPHA = (2 * DEPTH) ** 0.25
LN_EPS = 1e-5
GN_EPS = 64e-5
NORM_EPS = 1e-12
PAGE = 128
LANE = 128
SUBLANE = 8
T_PAD = SUBLANE
VMEM_LIMIT = 56 * 1024 * 1024


def _cparams(sem):
    return pltpu.CompilerParams(dimension_semantics=sem, vmem_limit_bytes=VMEM_LIMIT)


def _softplus(z):
    return jnp.maximum(z, 0.0) + jnp.log1p(jnp.exp(-jnp.abs(z)))


def _sigmoid(z):
    return 1.0 / (1.0 + jnp.exp(-z))


def _dot(a, b):
    return jnp.dot(a, b, preferred_element_type=F32)


def _dot_nt(a, b):
    return lax.dot_general(a, b, (((1,), (1,)), ((), ())), preferred_element_type=F32)


def _bf16_pieces(a, n):
    pieces = []
    for _ in range(n):
        p = a.astype(BF16)
        pieces.append(p)
        a = a - p.astype(F32)
    return pieces


def _dot_f32(a, b, ca=1, cb=0):
    ah, al = _bf16_pieces(a, 2)
    bh, bl = _bf16_pieces(b, 2)
    return lax.dot_general(jnp.concatenate([ah, ah, al], axis=ca), jnp.concatenate([bh, bl, bh], axis=cb),
                           (((ca,), (cb,)), ((), ())), preferred_element_type=F32)


def _dot_bf16(a, b, ca=1, cb=0):
    return lax.dot_general(a.astype(BF16), b.astype(BF16), (((ca,), (cb,)), ((), ())), preferred_element_type=F32)


def _dot_mask_f32(mask, b):
    m = mask.astype(BF16)
    return _dot(jnp.concatenate([m, m, m], axis=1), jnp.concatenate(_bf16_pieces(b, 3), axis=0))


def _layer_norm(y, g, b):
    mu = jnp.mean(y, axis=-1, keepdims=True)
    yc = y - mu
    var = jnp.mean(yc * yc, axis=-1, keepdims=True)
    return yc * lax.rsqrt(var + LN_EPS) * g + b


_MM_COLS = 256
_SB_SCALE = HEAD ** -0.5
_IN_GROUPS = ((N_SHIFT, True, False, 1.0),
              (W_A, False, True, _SB_SCALE),
              (W_A, True, True, 1.0),
              (W_A, True, True, 1.0),
              (W_C, True, False, 1.0),
              (N_GATE, True, False, 1.0))


_IN_KV_GROUPS = (2, 3)


def _in_proj_body(x_ref, w_ref, *out_refs, kv_transposed):
    x = x_ref[...].astype(BF16)
    refs = iter(out_refs)
    off = 0
    for gi, (width, want_f32, want_bf16, scale) in enumerate(_IN_GROUPS):
        ref_f32 = next(refs) if want_f32 else None
        ref_bf16 = next(refs) if want_bf16 else None
        for c in range(0, width, _MM_COLS):
            cols = slice(c, c + _MM_COLS)
            y = _dot(x, w_ref[:, off + c:off + c + _MM_COLS])
            if want_f32 and kv_transposed and gi in _IN_KV_GROUPS:
                ref_f32[cols, :] = y.T
            elif want_f32:
                ref_f32[:, cols] = y
            if want_bf16:
                ref_bf16[:, cols] = (y * scale).astype(BF16)
        off += width


def _in_proj(x, w_in_bf16, kv_seq=None):
    n = x.shape[0]
    bm = min(n, 256)
    row_spec = lambda w: pl.BlockSpec((bm, w), lambda i: (i, 0))
    specs, shapes = [], []
    for gi, (width, want_f32, want_bf16, _) in enumerate(_IN_GROUPS):
        if want_f32 and kv_seq is not None and gi in _IN_KV_GROUPS:
            n_seq, t_total = kv_seq
            nt = t_total // bm
            specs.append(pl.BlockSpec((width, bm), lambda i: (i // nt, i % nt)))
            shapes.append(jax.ShapeDtypeStruct((n_seq * width, t_total), F32))
        elif want_f32:
            specs.append(row_spec(width))
            shapes.append(jax.ShapeDtypeStruct((n, width), F32))
        if want_bf16:
            specs.append(row_spec(width))
            shapes.append(jax.ShapeDtypeStruct((n, width), BF16))
    return pl.pallas_call(
        functools.partial(_in_proj_body, kv_transposed=kv_seq is not None),
        grid=(n // bm,),
        in_specs=[pl.BlockSpec((bm, D_MODEL), lambda i: (i, 0)),
                  pl.BlockSpec((D_MODEL, N_IN), lambda i: (0, 0), pipeline_mode=pl.Buffered(1))],
        out_specs=specs,
        out_shape=shapes,
        compiler_params=_cparams(("parallel",)),
        name="in_proj",
    )(x, w_in_bf16)


def _matmul_body(x_ref, w_ref, o_ref):
    o_ref[...] = _dot(x_ref[...].astype(BF16), w_ref[...])


def _matmul(x, w_bf16, bn=512):
    n, k = x.shape
    m = w_bf16.shape[1]
    bm = min(n, 256)
    return pl.pallas_call(
        _matmul_body,
        grid=(n // bm, m // bn),
        in_specs=[pl.BlockSpec((bm, k), lambda i, j: (i, 0)), pl.BlockSpec((k, bn), lambda i, j: (0, j))],
        out_specs=pl.BlockSpec((bm, bn), lambda i, j: (i, j)),
        out_shape=jax.ShapeDtypeStruct((n, m), F32),
        compiler_params=_cparams(("parallel", "parallel")),
        name="matmul",
    )(x, w_bf16)


def _head_sum(x):
    n = x.shape[-1]
    same = (lax.broadcasted_iota(jnp.int32, (n, n), 0) // HEAD) == (lax.broadcasted_iota(jnp.int32, (n, n), 1) // HEAD)
    same = same.astype(BF16)
    return _dot(jnp.concatenate(_bf16_pieces(x, 3), axis=1), jnp.concatenate([same, same, same], axis=0))


def _rwkv_prep_body(p_ref, sh0_ref, mu_ref, w0_ref, wd2_ref, a0_ref, wa2_ref, wg2_ref, kk_ref, ka_ref, rk_ref,
                    r_o, lw_o, k_o, v_o, kn_o, bb_o, g_o, bonus_o, shift_o, carry, *, tb, t_valid, t_total):
    t = pl.program_id(1)

    @pl.when(t == 0)
    def _():
        carry[...] = sh0_ref[0]

    p = p_ref[...]
    row = lax.broadcasted_iota(jnp.int32, (tb, 1), 0)
    prev = jnp.where(row == 0, carry[...], pltpu.roll(p, 1, 0))
    carry[...] = p[tb - 1:tb, :]

    last = t_valid - 1
    @pl.when(t == last // tb)
    def _():
        shift_o[0] = p[last % tb:last % tb + 1, :]

    z = p + mu_ref[...] * (prev - p)
    r = z[:, 0:W_A]
    k = z[:, W_A:2 * W_A]
    v = z[:, 2 * W_A:3 * W_A]
    c0 = 3 * W_A
    wl = z[:, c0:c0 + R_DECAY]
    al = z[:, c0 + R_DECAY:c0 + R_DECAY + R_AAA]
    gl = z[:, c0 + R_DECAY + R_AAA:N_SHIFT]

    w_log = -_softplus(-(w0_ref[...] + _dot(jnp.tanh(wl).astype(BF16), wd2_ref[...]))) - 0.5
    lw = -jnp.exp(w_log)
    a = _sigmoid(a0_ref[...] + _dot(al.astype(BF16), wa2_ref[...]))
    g = _dot(_sigmoid(gl).astype(BF16), wg2_ref[...])
    kn = k * kk_ref[...]
    kn = kn / jnp.maximum(jnp.sqrt(_head_sum(kn * kn)), NORM_EPS)
    k2 = k * (1.0 + (a - 1.0) * ka_ref[...])
    bonus = _head_sum(r * k2 * rk_ref[...]) * v
    bb = kn * a
    if t_valid < t_total:
        ok = (t * tb + row) < t_valid
        lw = jnp.where(ok, lw, 0.0)
        kn = jnp.where(ok, kn, 0.0)
        bb = jnp.where(ok, bb, 0.0)
        k2 = jnp.where(ok, k2, 0.0)
        v = jnp.where(ok, v, 0.0)
    r_o[...] = r
    lw_o[...] = lw
    k_o[...] = k2
    v_o[...] = v
    kn_o[...] = kn
    bb_o[...] = bb
    g_o[...] = g
    bonus_o[...] = bonus


def _rwkv_prep(p_shift, shift0, lw, *, n_seq, t_total, t_valid):
    tb = min(t_total, 256)
    nt = t_total // tb
    n = n_seq * t_total
    row_spec = lambda w: pl.BlockSpec((tb, w), lambda b, t: (b * nt + t, 0))
    par = lambda a: pl.BlockSpec(a.shape, lambda b, t: (0,) * a.ndim)
    params = (lw['mu_shift'], lw['w0_decay'], lw['w_decay2'], lw['a0'], lw['w_a2'], lw['w_g2'],
              lw['k_k'], lw['k_a'], lw['r_k'])
    outs = pl.pallas_call(
        functools.partial(_rwkv_prep_body, tb=tb, t_valid=t_valid, t_total=t_total),
        grid=(n_seq, nt),
        in_specs=[row_spec(N_SHIFT), pl.BlockSpec((1, 1, N_SHIFT), lambda b, t: (b, 0, 0))] + [par(a) for a in params],
        out_specs=[row_spec(W_A)] * 8 + [pl.BlockSpec((1, 1, N_SHIFT), lambda b, t: (b, 0, 0))],
        out_shape=[jax.ShapeDtypeStruct((n, W_A), F32)] * 8 + [jax.ShapeDtypeStruct((n_seq, 1, N_SHIFT), F32)],
        scratch_shapes=[pltpu.VMEM((1, N_SHIFT), F32)],
        compiler_params=_cparams(("parallel", "arbitrary")),
        name="rwkv_prep",
    )(p_shift, shift0.reshape(n_seq, 1, N_SHIFT), *params)
    return outs[:8], outs[8].reshape(n_seq, N_SHIFT)


_INV_BLOCK = 16


def _nilpotent_inverses(ns, steps):
    invs = [-n for n in ns]
    sqs = list(ns)
    for _ in range(steps - 1):
        sqs = [_dot_bf16(q, q) for q in sqs]
        invs = [i + q + _dot_bf16(i, q) for i, q in zip(invs, sqs)]
    return invs


def _unit_lower_inverses_minus_i(ns, c):
    blk = min(_INV_BLOCK, c)
    if blk == c:
        return _nilpotent_inverses(ns, int(np.log2(c)))
    ri = lax.broadcasted_iota(jnp.int32, (c, c), 0)
    ci = lax.broadcasted_iota(jnp.int32, (c, c), 1)
    on_diag = (ri // blk) == (ci // blk)
    eye = (ri == ci).astype(F32)
    d_invs = [d + eye for d in _nilpotent_inverses([jnp.where(on_diag, n, 0.0) for n in ns], int(np.log2(blk)))]
    ms = [_dot_bf16(d, jnp.where(on_diag, 0.0, n)) for d, n in zip(d_invs, ns)]
    m_invs = [m + eye for m in _nilpotent_inverses(ms, int(np.log2(c // blk)))]
    return [_dot_bf16(m, d) - eye for m, d in zip(m_invs, d_invs)]


def _wkv_body(r_ref, lw_ref, k_ref, v_ref, kn_ref, bb_ref, g_ref, bonus_ref, s0_ref, gng_ref, gnb_ref,
              o_ref, sT_ref, state, o_scr, *, c):
    ci = pl.program_id(1)

    @pl.when(ci == 0)
    def _():
        state[...] = s0_ref[0]

    ri = lax.broadcasted_iota(jnp.int32, (c, c), 0)
    cj = lax.broadcasted_iota(jnp.int32, (c, c), 1)
    incl = ri >= cj
    strict = ri > cj

    lw = lw_ref[...]
    cs = _dot_mask_f32(incl, lw)
    tot = cs[c - 1:c, :]
    e_out = jnp.exp(-cs)
    e_end = jnp.exp(tot - cs)
    k = k_ref[...]
    bb = bb_ref[...]
    r_t = r_ref[...] * jnp.exp(cs)
    kap = kn_ref[...] * jnp.exp(cs - lw)
    k_t = k * e_out
    b_t = bb * e_out
    k_d = k * e_end
    b_d = bb * e_end
    e_tot = jnp.exp(tot)
    v = v_ref[...]

    heads = [slice(h * HEAD, (h + 1) * HEAD) for h in range(N_HEAD)]
    kap_h = [kap[:, sl] for sl in heads]
    v_h = [v[:, sl] for sl in heads]
    kr = [jnp.concatenate([kp, r_t[:, sl]], axis=0) for kp, sl in zip(kap_h, heads)]
    g_k = [_dot_bf16(x, k_t[:, sl], 1, 1) for x, sl in zip(kr, heads)]
    g_b = [_dot_bf16(x, b_t[:, sl], 1, 1) for x, sl in zip(kr, heads)]
    a_kk = [jnp.where(strict, g[:c], 0.0) for g in g_k]
    a_kr = [jnp.where(incl, g[c:], 0.0) for g in g_k]
    a_bk = [jnp.where(strict, g[:c], 0.0) for g in g_b]
    a_br = [jnp.where(incl, g[c:], 0.0) for g in g_b]
    t_inv = _unit_lower_inverses_minus_i(a_bk, c)
    rhs0 = [_dot_bf16(a, x) for a, x in zip(a_kk, v_h)]
    u0 = [x + _dot_bf16(t, x) for t, x in zip(t_inv, rhs0)]
    w1 = [x + _dot_bf16(t, x) for t, x in zip(t_inv, kap_h)]
    s_h = [state[h] for h in range(N_HEAD)]
    su = [_dot_bf16(jnp.concatenate([w, r_t[:, sl]], axis=0), s, 1, 1) for w, sl, s in zip(w1, heads, s_h)]
    u = [x[:c] + y for x, y in zip(su, u0)]
    o = [x[c:] + _dot_bf16(a, y) - _dot_bf16(b, z) for x, a, y, b, z in zip(su, a_kr, v_h, a_br, u)]
    for h, sl in enumerate(heads):
        vu = jnp.concatenate([v_h[h], -u[h]], axis=0)
        kb = jnp.concatenate([k_d[:, sl], b_d[:, sl]], axis=0)
        state[h] = s_h[h] * e_tot[:, sl] + _dot_bf16(vu, kb, 0, 0)
    for h, sl in enumerate(heads):
        mu = jnp.mean(o[h], axis=-1, keepdims=True)
        oc = o[h] - mu
        var = jnp.mean(oc * oc, axis=-1, keepdims=True)
        o_scr[:, sl] = oc * lax.rsqrt(var + GN_EPS)

    o_ref[...] = (o_scr[...] * gng_ref[...] + gnb_ref[...] + bonus_ref[...]) * g_ref[...]

    @pl.when(ci == pl.num_programs(1) - 1)
    def _():
        sT_ref[0] = state[...]


def _wkv(prep, state0, gn_g, gn_b, *, n_seq, t_total):
    c = min(t_total, 64)
    nc = t_total // c
    n = n_seq * t_total
    row = pl.BlockSpec((c, W_A), lambda b, i: (b * nc + i, 0))
    st = pl.BlockSpec((1, N_HEAD, HEAD, HEAD), lambda b, i: (b, 0, 0, 0))
    par = pl.BlockSpec((1, W_A), lambda b, i: (0, 0))
    return pl.pallas_call(
        functools.partial(_wkv_body, c=c),
        grid=(n_seq, nc),
        in_specs=[row] * 8 + [st, par, par],
        out_specs=[row, st],
        out_shape=[jax.ShapeDtypeStruct((n, W_A), F32), jax.ShapeDtypeStruct((n_seq, N_HEAD, HEAD, HEAD), F32)],
        scratch_shapes=[pltpu.VMEM((N_HEAD, HEAD, HEAD), F32), pltpu.VMEM((c, W_A), F32)],
        compiler_params=_cparams(("parallel", "arbitrary")),
        name="wkv",
    )(*prep, state0, gn_g, gn_b)


def _later_key_mask(n):
    j = lax.broadcasted_iota(jnp.int32, (n, n), 0)
    s = lax.broadcasted_iota(jnp.int32, (n, n), 1)
    return jnp.where(j > s, -1.0, 0.0).astype(BF16)


_SP_LINEAR = 20.0


def _sb_scores(z, carry, later, visible=None):
    sp = jnp.where(z > _SP_LINEAR, z, jnp.log(1.0 + jnp.exp(jnp.minimum(z, _SP_LINEAR))))
    if visible is not None:
        sp = jnp.where(visible, sp, 0.0)
    tail = _dot(sp.astype(BF16), later)
    att = jnp.exp(z - sp + tail + carry)
    if visible is not None:
        att = jnp.where(visible, att, 0.0)
    return att, carry + (tail - sp)[:, 0:1]


_SB_TQ = 512
_SB_RQ = 512
_SB_KW = 256


def _sb_prompt_body(qi_ref, kj_ref, q_ref, k_ref, v_ref, bias_ref, o_ref, acc, carry, *, tq, rq, kw):
    s = pl.program_id(1)
    qi = qi_ref[s]
    kj = kj_ref[s]
    later = _later_key_mask(kw)

    @pl.when(kj == qi)
    def _():
        acc[...] = jnp.zeros_like(acc)
        carry[...] = jnp.zeros_like(carry)

    def sweep(diagonal):
        masks = {}

        def visible_mask(qs, ks):
            if (qs, ks) not in masks:
                row_id = qs * rq + lax.broadcasted_iota(jnp.int32, (rq, kw), 0)
                col_id = ks * kw + lax.broadcasted_iota(jnp.int32, (rq, kw), 1)
                masks[qs, ks] = col_id < row_id
            return masks[qs, ks]

        for h in range(N_HEAD):
            sl = slice(h * HEAD, (h + 1) * HEAD)
            bias = bias_ref[h]
            for qs in range(tq // rq):
                rows = slice(qs * rq, (qs + 1) * rq)
                q = q_ref[rows, sl]
                cr = carry[h, rows, :]
                a = acc[rows, sl]
                for ks in reversed(range(tq // kw)):
                    visible = None
                    if diagonal:
                        if ks * kw >= (qs + 1) * rq:
                            continue
                        if (ks + 1) * kw > qs * rq:
                            visible = visible_mask(qs, ks)
                    keys = slice(ks * kw, (ks + 1) * kw)
                    att, cr = _sb_scores(_dot_nt(q, k_ref[keys, sl]) + bias, cr, later, visible)
                    a = a + _dot(att.astype(BF16), v_ref[keys, sl])
                acc[rows, sl] = a
                carry[h, rows, :] = cr

    @pl.when(kj == qi)
    def _():
        sweep(True)

    @pl.when(kj != qi)
    def _():
        sweep(False)

    @pl.when(kj == 0)
    def _():
        o_ref[...] = acc[...]


def _sb_prompt(q, k, v, bias, *, n_seq, t_total):
    tq = min(_SB_TQ, t_total)
    rq = min(_SB_RQ, tq)
    kw = min(_SB_KW, tq)
    nq = t_total // tq
    pairs = [(i, j) for i in range(nq) for j in range(i, -1, -1)]
    qi = jnp.asarray([p[0] for p in pairs], jnp.int32)
    kj = jnp.asarray([p[1] for p in pairs], jnp.int32)
    n = n_seq * t_total
    return pl.pallas_call(
        functools.partial(_sb_prompt_body, tq=tq, rq=rq, kw=kw),
        grid_spec=pltpu.PrefetchScalarGridSpec(
            num_scalar_prefetch=2,
            grid=(n_seq, len(pairs)),
            in_specs=[pl.BlockSpec((tq, W_A), lambda b, s, qi, kj: (b * nq + qi[s], 0)),
                      pl.BlockSpec((tq, W_A), lambda b, s, qi, kj: (b * nq + kj[s], 0)),
                      pl.BlockSpec((tq, W_A), lambda b, s, qi, kj: (b * nq + kj[s], 0)),
                      pl.BlockSpec(memory_space=pltpu.SMEM)],
            out_specs=pl.BlockSpec((tq, W_A), lambda b, s, qi, kj: (b * nq + qi[s], 0)),
            scratch_shapes=[pltpu.VMEM((tq, W_A), F32), pltpu.VMEM((N_HEAD, tq, 1), F32)]),
        out_shape=jax.ShapeDtypeStruct((n, W_A), F32),
        compiler_params=_cparams(("parallel", "arbitrary")),
        name="sb_prompt",
    )(qi, kj, q, k, v, bias)


_SB_PAGES = 8


def _sb_sample_body(pt_ref, q_ref, kn_ref, vn_ref, bias_ref, *refs):
    k_refs = refs[:_SB_PAGES]
    v_refs = refs[_SB_PAGES:2 * _SB_PAGES]
    o_ref, qx, bias_col, acc, carry = refs[2 * _SB_PAGES:]
    j = pl.program_id(1)
    m = N_HEAD * T_PAD
    later = _later_key_mask(PAGE)

    def page(k_ref, v_ref, visible=None):
        z = _dot(qx[...], k_ref[0].astype(BF16)) + bias_col[...]
        att, cr = _sb_scores(z, carry[...], later, visible)
        carry[...] = cr
        acc[...] += _dot_nt(att.astype(BF16), v_ref[0].astype(BF16))

    @pl.when(j == 0)
    def _():
        row_head = lax.broadcasted_iota(jnp.int32, (m, 1), 0) // T_PAD
        col_head = lax.broadcasted_iota(jnp.int32, (m, W_A), 1) // HEAD
        qx[...] = jnp.where(col_head == row_head, jnp.concatenate([q_ref[0]] * N_HEAD, axis=0), 0.0).astype(BF16)
        b = jnp.zeros((m, 1), F32)
        for h in range(N_HEAD):
            b = jnp.where(row_head == h, bias_ref[h], b)
        bias_col[...] = b
        acc[...] = jnp.zeros_like(acc)
        carry[...] = jnp.zeros_like(carry)
        tok = lax.broadcasted_iota(jnp.int32, (m, PAGE), 0) % T_PAD
        key = lax.broadcasted_iota(jnp.int32, (m, PAGE), 1)
        page(kn_ref, vn_ref, key < tok)

    for r in range(_SB_PAGES):
        page(k_refs[r], v_refs[r])

    @pl.when(j == pl.num_programs(1) - 1)
    def _():
        a = acc[...]
        col_head = lax.broadcasted_iota(jnp.int32, (T_PAD, W_A), 1) // HEAD
        o = jnp.zeros((T_PAD, W_A), F32)
        for h in range(N_HEAD):
            o = o + jnp.where(col_head == h, a[h * T_PAD:(h + 1) * T_PAD, :], 0.0)
        o_ref[0] = o


def _sb_sample(q, k_new, v_new, bias, k_cache, v_cache, page_table, layer):
    n_seq, n_pages = page_table.shape
    depth, n_pool = k_cache.shape[:2]
    as_pages = lambda c: c.transpose(0, 1, 3, 4, 2).reshape(depth * n_pool, W_A, PAGE)

    def new_page(x):
        x = x.reshape(n_seq, T_PAD, W_A).transpose(0, 2, 1)
        return jnp.pad(x, ((0, 0), (0, 0), (0, PAGE - T_PAD)))

    page_block = (1, W_A, PAGE)
    seq_page = pl.BlockSpec(page_block, lambda b, j, pt: (b, 0, 0))

    def page_spec(r):
        return pl.BlockSpec(page_block,
                            lambda b, j, pt: (layer * n_pool + pt[b * n_pages + n_pages - 1 - (j * _SB_PAGES + r)], 0, 0))

    m = N_HEAD * T_PAD
    out = pl.pallas_call(
        _sb_sample_body,
        grid_spec=pltpu.PrefetchScalarGridSpec(
            num_scalar_prefetch=1,
            grid=(n_seq, n_pages // _SB_PAGES),
            in_specs=[pl.BlockSpec((1, T_PAD, W_A), lambda b, j, pt: (b, 0, 0)), seq_page, seq_page,
                      pl.BlockSpec(memory_space=pltpu.SMEM)] + [page_spec(r) for r in range(_SB_PAGES)] * 2,
            out_specs=pl.BlockSpec((1, T_PAD, W_A), lambda b, j, pt: (b, 0, 0)),
            scratch_shapes=[pltpu.VMEM((m, W_A), BF16), pltpu.VMEM((m, 1), F32),
                            pltpu.VMEM((m, W_A), F32), pltpu.VMEM((m, 1), F32)]),
        out_shape=jax.ShapeDtypeStruct((n_seq, T_PAD, W_A), F32),
        compiler_params=_cparams(("parallel", "arbitrary")),
        name="sb_sample",
    )(page_table.reshape(-1), q.reshape(n_seq, T_PAD, W_A), new_page(k_new), new_page(v_new), bias,
      *([as_pages(k_cache)] * _SB_PAGES), *([as_pages(v_cache)] * _SB_PAGES))
    return out.reshape(n_seq * T_PAD, W_A)


def _mem_attn_body(q_ref, k_ref, v_ref, o_ref):
    for h in range(H_C):
        sl = slice(h * HD_C, (h + 1) * HD_C)
        s = _dot_nt(q_ref[:, sl].astype(BF16), k_ref[0, :, sl].astype(BF16)) * (HD_C ** -0.5)
        e = jnp.exp(s - jnp.max(s, axis=-1, keepdims=True))
        prob = e / jnp.sum(e, axis=-1, keepdims=True)
        o_ref[:, sl] = _dot(prob.astype(BF16), v_ref[0, :, sl].astype(BF16))


def _mem_attn(q, mem_k, mem_v, *, n_seq, t_total):
    tq = min(t_total, 512)
    nq = t_total // tq
    mem = pl.BlockSpec((1, N_MEM, W_C), lambda b, i: (b, 0, 0))
    row = pl.BlockSpec((tq, W_C), lambda b, i: (b * nq + i, 0))
    return pl.pallas_call(
        _mem_attn_body,
        grid=(n_seq, nq),
        in_specs=[row, mem, mem],
        out_specs=row,
        out_shape=jax.ShapeDtypeStruct((n_seq * t_total, W_C), F32),
        compiler_params=_cparams(("parallel", "parallel")),
        name="mem_attn",
    )(q, mem_k, mem_v)


def _combine_body(x_ref, oa_ref, ob_ref, oc_ref, gate_ref, wa_ref, wb_ref, wc_ref, wo_ref, g_ref, b_ref, y_ref):
    h = None
    for i, (o_ref, w_ref) in enumerate(((oa_ref, wa_ref), (ob_ref, wb_ref), (oc_ref, wc_ref))):
        part = _sigmoid(gate_ref[:, i * D_MODEL:(i + 1) * D_MODEL]) * _dot(o_ref[...].astype(BF16), w_ref[...])
        h = part if h is None else h + part
    y = DN_ALPHA * x_ref[...] + _dot(h.astype(BF16), wo_ref[...])
    y_ref[...] = _layer_norm(y, g_ref[...], b_ref[...])


def _combine(x, o_a, o_b, o_c, gate, wa, wb, wc, wo, ln_g, ln_b):
    n = x.shape[0]
    bm = min(n, 256)
    row = lambda w: pl.BlockSpec((bm, w), lambda i: (i, 0))
    const = lambda a: pl.BlockSpec(a.shape, lambda i: (0,) * a.ndim, pipeline_mode=pl.Buffered(1))
    return pl.pallas_call(
        _combine_body,
        grid=(n // bm,),
        in_specs=[row(D_MODEL), row(W_A), row(W_A), row(W_C), row(N_GATE)] + [const(a) for a in (wa, wb, wc, wo, ln_g, ln_b)],
        out_specs=row(D_MODEL),
        out_shape=jax.ShapeDtypeStruct((n, D_MODEL), F32),
        compiler_params=_cparams(("parallel",)),
        name="combine",
    )(x, o_a, o_b, o_c, gate, wa, wb, wc, wo, ln_g, ln_b)


def _top2(scores, ids, lowest):
    m1 = jnp.max(scores, axis=0, keepdims=True)
    i1 = jnp.min(jnp.where(scores == m1, ids, N_EXPERTS), axis=0, keepdims=True)
    rest = jnp.where(ids == i1, lowest, scores)
    m2 = jnp.max(rest, axis=0, keepdims=True)
    i2 = jnp.min(jnp.where(rest == m2, ids, N_EXPERTS), axis=0, keepdims=True)
    return m1, i1, m2, i2


def _router_gates(x, wr_t, br):
    m = x.shape[0]
    logits = _dot_f32(wr_t, x, 1, 1) + br
    e = jnp.exp(logits - jnp.max(logits, axis=0, keepdims=True))
    scores = e / jnp.sum(e, axis=0, keepdims=True)
    ids = lax.broadcasted_iota(jnp.int32, (N_EXPERTS, m), 0)
    grp = ids // E_PER_GROUP
    best = None
    for g in range(N_GROUPS):
        m1, _, m2, _ = _top2(jnp.where(grp == g, scores, -1.0), ids, -2.0)
        gs = m1 + m2
        if best is None:
            best, sel = gs, jnp.zeros((1, m), jnp.int32)
        else:
            sel = jnp.where(gs > best, g, sel)
            best = jnp.maximum(gs, best)
    m1, i1, m2, i2 = _top2(jnp.where(grp == sel, scores, -1.0), ids, -2.0)
    tot = m1 + m2
    return jnp.where(ids == i1, m1 / tot, 0.0) + jnp.where(ids == i2, m2 / tot, 0.0)


def _moe_body(x_ref, wr_ref, br_ref, w1_ref, w3_ref, w2_ref, g_ref, b_ref, y_ref, gate_scr, acc):
    e = pl.program_id(1)
    x = x_ref[...]

    @pl.when(e == 0)
    def _():
        gate_scr[...] = _router_gates(x, wr_ref[...], br_ref[...]).T
        acc[...] = jnp.zeros_like(acc)

    lane = lax.broadcasted_iota(jnp.int32, gate_scr.shape, 1)
    ge = jnp.sum(jnp.where(lane == e, gate_scr[...], 0.0), axis=1, keepdims=True)
    xb = x.astype(BF16)
    u = _dot(xb, w1_ref[0])
    h = (u * _sigmoid(u)) * _dot(xb, w3_ref[0])
    acc[...] += _dot((h * ge).astype(BF16), w2_ref[0])

    @pl.when(e == pl.num_programs(1) - 1)
    def _():
        y_ref[...] = _layer_norm(DN_ALPHA * x + acc[...], g_ref[...], b_ref[...])


def _moe(x, wr_t, br, w1, w3, w2, ln_g, ln_b):
    n = x.shape[0]
    bm = min(n, 1024)
    const = lambda a: pl.BlockSpec(a.shape, lambda i, e: (0,) * a.ndim)
    return pl.pallas_call(
        _moe_body,
        grid=(n // bm, N_EXPERTS),
        in_specs=[pl.BlockSpec((bm, D_MODEL), lambda i, e: (i, 0)), const(wr_t), const(br),
                  pl.BlockSpec((1, D_MODEL, D_FF), lambda i, e: (e, 0, 0)),
                  pl.BlockSpec((1, D_MODEL, D_FF), lambda i, e: (e, 0, 0)),
                  pl.BlockSpec((1, D_FF, D_MODEL), lambda i, e: (e, 0, 0)),
                  const(ln_g), const(ln_b)],
        out_specs=pl.BlockSpec((bm, D_MODEL), lambda i, e: (i, 0)),
        out_shape=jax.ShapeDtypeStruct((n, D_MODEL), F32),
        scratch_shapes=[pltpu.VMEM((bm, N_EXPERTS), F32), pltpu.VMEM((bm, D_MODEL), F32)],
        compiler_params=_cparams(("parallel", "arbitrary")),
        name="moe",
    )(x, wr_t, br, w1, w3, w2, ln_g, ln_b)


def _trunk_layer(x, lw, shared, sb_fn, mem_k, mem_v, wkv0, shift0, *, n_seq, t_total, t_valid, kv_channel_major):
    p_shift, q16, k_b, k16, v_b, v16, q_c, gate = _in_proj(x, lw['w_in'], (n_seq, t_total) if kv_channel_major else None)
    prep, shift = _rwkv_prep(p_shift, shift0, lw, n_seq=n_seq, t_total=t_total, t_valid=t_valid)
    o_a, wkv = _wkv(prep, wkv0, lw['gn_g'], lw['gn_b'], n_seq=n_seq, t_total=t_total)
    o_b = sb_fn(q16, k_b, k16, v_b, v16)
    o_c = _mem_attn(q_c, mem_k, mem_v, n_seq=n_seq, t_total=t_total)
    x = _combine(x, o_a, o_b, o_c, gate, lw['w_br_a'], lw['w_br_b'], lw['w_br_c'], lw['w_out'], lw['ln1_g'], lw['ln1_b'])
    x = _moe(x, shared['wr_t'], shared['br'], lw['w1'], lw['w3'], lw['w2'], lw['ln2_g'], lw['ln2_b'])
    return x, k_b, v_b, wkv, shift


def kernel(x_prompt, x_sample, mem_prompt, cache_sb_k, cache_sb_v, cache_mem_k, cache_mem_v, state_wkv, state_shift, page_table, w_in, mu_shift, w0_decay, w_decay2, a0, w_a2, w_g2, k_k, k_a, r_k, gn_g, gn_b, sb_bias, w_mem_kv, w_br_a, w_br_b, w_br_c, w_out, ln1_g, ln1_b, w_router, b_router, w1, w3, w2, ln2_g, ln2_b):
    depth = w_in.shape[0]
    n_p, t_p = x_prompt.shape[:2]
    n_s, t_s = x_sample.shape[:2]
    n_pool = cache_sb_k.shape[1]
    row = lambda a, l: a[l].reshape(1, -1)
    shared = {'wr_t': w_router.T, 'br': b_router.reshape(-1, 1)}

    xp = x_prompt.reshape(n_p * t_p, D_MODEL)
    xs = jnp.pad(x_sample, ((0, 0), (0, T_PAD - t_s), (0, 0))).reshape(n_s * T_PAD, D_MODEL)
    mem_rows = mem_prompt.reshape(n_p * N_MEM, D_MODEL)
    zero_wkv = jnp.zeros((n_p, N_HEAD, HEAD, HEAD), F32)
    zero_shift = jnp.zeros((n_p, N_SHIFT), F32)

    outs = [[] for _ in range(10)]
    for l in range(depth):
        lw = {'w_in': w_in[l].astype(BF16), 'mu_shift': row(mu_shift, l), 'w0_decay': row(w0_decay, l),
              'w_decay2': w_decay2[l].astype(BF16), 'a0': row(a0, l), 'w_a2': w_a2[l].astype(BF16),
              'w_g2': w_g2[l].astype(BF16), 'k_k': row(k_k, l), 'k_a': row(k_a, l), 'r_k': row(r_k, l),
              'gn_g': row(gn_g, l), 'gn_b': row(gn_b, l),
              'w_br_a': w_br_a[l].astype(BF16), 'w_br_b': w_br_b[l].astype(BF16), 'w_br_c': w_br_c[l].astype(BF16),
              'w_out': w_out[l].astype(BF16), 'ln1_g': row(ln1_g, l), 'ln1_b': row(ln1_b, l),
              'w1': w1[l].astype(BF16), 'w3': w3[l].astype(BF16), 'w2': w2[l].astype(BF16),
              'ln2_g': row(ln2_g, l), 'ln2_b': row(ln2_b, l)}
        bias = sb_bias[l]

        mem_kv = _matmul(mem_rows, w_mem_kv[l].astype(BF16))
        mk = mem_kv[:, :W_C].reshape(n_p, N_MEM, W_C)
        mv = mem_kv[:, W_C:].reshape(n_p, N_MEM, W_C)
        sb_p = functools.partial(_sb_prompt, bias=bias, n_seq=n_p, t_total=t_p)
        xp, kb, vb, wkv, sh = _trunk_layer(xp, lw, shared, lambda q16, k, k16, v, v16: sb_p(q16, k16, v16), mk, mv,
                                           zero_wkv, zero_shift, n_seq=n_p, t_total=t_p, t_valid=t_p, kv_channel_major=True)
        for i, a in zip((0, 1, 4, 5, 6, 8), (kb.reshape(n_p, N_HEAD, HEAD, t_p), vb.reshape(n_p, N_HEAD, HEAD, t_p),
                                             mk.reshape(n_p, N_MEM, H_C, HD_C), mv.reshape(n_p, N_MEM, H_C, HD_C), wkv, sh)):
            outs[i].append(a)

        def sb_s(q16, k, k16, v, v16):
            return _sb_sample(q16.astype(F32), k, v, bias, cache_sb_k, cache_sb_v, page_table, l)

        xs, kb, vb, wkv, sh = _trunk_layer(xs, lw, shared, sb_s, cache_mem_k[l].reshape(n_s, N_MEM, W_C),
                                           cache_mem_v[l].reshape(n_s, N_MEM, W_C), state_wkv[l], state_shift[l],
                                           n_seq=n_s, t_total=T_PAD, t_valid=t_s, kv_channel_major=False)
        for i, a in zip((2, 3, 7, 9), (kb.reshape(n_s, T_PAD, N_HEAD, HEAD)[:, :t_s], vb.reshape(n_s, T_PAD, N_HEAD, HEAD)[:, :t_s],
                                       wkv, sh)):
            outs[i].append(a)

    st = [jnp.stack(o) for o in outs]
    for i in (0, 1):
        st[i] = st[i].transpose(0, 1, 4, 2, 3)
    y_p = xp.reshape(n_p, t_p, D_MODEL)
    y_s = xs.reshape(n_s, T_PAD, D_MODEL)[:, :t_s]
    return (y_p, y_s, st[0], st[1], st[2], st[3], st[4], st[5], st[6], st[7], st[8], st[9])
```

```python
import functools

import numpy as np
import jax
import jax.numpy as jnp
from jax import lax
from jax.experimental import pallas as pl
from jax.experimental.pallas import tpu as pltpu

F32 = jnp.float32
BF16 = jnp.bfloat16

D_MODEL = 1024
DEPTH = 4
N_HEAD = 8
HEAD = 64
W_A = N_HEAD * HEAD
R_DECAY, R_AAA, R_GATE = 64, 64, 128
H_C, HD_C = 4, 128
W_C = H_C * HD_C
N_MEM = 256
N_SHIFT = 3 * W_A + R_DECAY + R_AAA + R_GATE
N_GATE = 3 * D_MODEL
N_IN = N_SHIFT + 3 * W_A + W_C + N_GATE
N_EXPERTS, N_GROUPS, E_PER_GROUP, D_FF = 16, 4, 4, 512
DN_ALPHA = (2 * DEPTH) ** 0.25
LN_EPS = 1e-5
GN_EPS = 64e-5
NORM_EPS = 1e-12
PAGE = 128
LANE = 128
SUBLANE = 8
T_PAD = SUBLANE
VMEM_LIMIT = 56 * 1024 * 1024


def _cparams(sem):
    return pltpu.CompilerParams(dimension_semantics=sem, vmem_limit_bytes=VMEM_LIMIT)


def _softplus(z):
    return jnp.maximum(z, 0.0) + jnp.log1p(jnp.exp(-jnp.abs(z)))


def _sigmoid(z):
    return 1.0 / (1.0 + jnp.exp(-z))


def _dot(a, b):
    return jnp.dot(a, b, preferred_element_type=F32)


def _dot_nt(a, b):
    return lax.dot_general(a, b, (((1,), (1,)), ((), ())), preferred_element_type=F32)


def _bf16_pieces(a, n):
    pieces = []
    for _ in range(n):
        p = a.astype(BF16)
        pieces.append(p)
        a = a - p.astype(F32)
    return pieces


def _dot_f32(a, b, ca=1, cb=0):
    ah, al = _bf16_pieces(a, 2)
    bh, bl = _bf16_pieces(b, 2)
    return lax.dot_general(jnp.concatenate([ah, ah, al], axis=ca), jnp.concatenate([bh, bl, bh], axis=cb),
                           (((ca,), (cb,)), ((), ())), preferred_element_type=F32)


def _dot_bf16(a, b, ca=1, cb=0):
    return lax.dot_general(a.astype(BF16), b.astype(BF16), (((ca,), (cb,)), ((), ())), preferred_element_type=F32)


def _dot_mask_f32(mask, b):
    m = mask.astype(BF16)
    return _dot(jnp.concatenate([m, m, m], axis=1), jnp.concatenate(_bf16_pieces(b, 3), axis=0))


def _layer_norm(y, g, b):
    mu = jnp.mean(y, axis=-1, keepdims=True)
    yc = y - mu
    var = jnp.mean(yc * yc, axis=-1, keepdims=True)
    return yc * lax.rsqrt(var + LN_EPS) * g + b


_MM_COLS = 256
_SB_SCALE = HEAD ** -0.5
_IN_GROUPS = ((N_SHIFT, True, False, 1.0),
              (W_A, False, True, _SB_SCALE),
              (W_A, True, True, 1.0),
              (W_A, True, True, 1.0),
              (W_C, True, False, 1.0),
              (N_GATE, True, False, 1.0))


_IN_KV_GROUPS = (2, 3)


def _in_proj_body(x_ref, w_ref, *out_refs, kv_transposed):
    x = x_ref[...].astype(BF16)
    refs = iter(out_refs)
    off = 0
    for gi, (width, want_f32, want_bf16, scale) in enumerate(_IN_GROUPS):
        ref_f32 = next(refs) if want_f32 else None
        ref_bf16 = next(refs) if want_bf16 else None
        for c in range(0, width, _MM_COLS):
            cols = slice(c, c + _MM_COLS)
            y = _dot(x, w_ref[:, off + c:off + c + _MM_COLS])
            if want_f32 and kv_transposed and gi in _IN_KV_GROUPS:
                ref_f32[cols, :] = y.T
            elif want_f32:
                ref_f32[:, cols] = y
            if want_bf16:
                ref_bf16[:, cols] = (y * scale).astype(BF16)
        off += width


def _in_proj(x, w_in_bf16, kv_seq=None):
    n = x.shape[0]
    bm = min(n, 256)
    row_spec = lambda w: pl.BlockSpec((bm, w), lambda i: (i, 0))
    specs, shapes = [], []
    for gi, (width, want_f32, want_bf16, _) in enumerate(_IN_GROUPS):
        if want_f32 and kv_seq is not None and gi in _IN_KV_GROUPS:
            n_seq, t_total = kv_seq
            nt = t_total // bm
            specs.append(pl.BlockSpec((width, bm), lambda i: (i // nt, i % nt)))
            shapes.append(jax.ShapeDtypeStruct((n_seq * width, t_total), F32))
        elif want_f32:
            specs.append(row_spec(width))
            shapes.append(jax.ShapeDtypeStruct((n, width), F32))
        if want_bf16:
            specs.append(row_spec(width))
            shapes.append(jax.ShapeDtypeStruct((n, width), BF16))
    return pl.pallas_call(
        functools.partial(_in_proj_body, kv_transposed=kv_seq is not None),
        grid=(n // bm,),
        in_specs=[pl.BlockSpec((bm, D_MODEL), lambda i: (i, 0)),
                  pl.BlockSpec((D_MODEL, N_IN), lambda i: (0, 0), pipeline_mode=pl.Buffered(1))],
        out_specs=specs,
        out_shape=shapes,
        compiler_params=_cparams(("parallel",)),
        name="in_proj",
    )(x, w_in_bf16)


def _matmul_body(x_ref, w_ref, o_ref):
    o_ref[...] = _dot(x_ref[...].astype(BF16), w_ref[...])


def _matmul(x, w_bf16, bn=512):
    n, k = x.shape
    m = w_bf16.shape[1]
    bm = min(n, 256)
    return pl.pallas_call(
        _matmul_body,
        grid=(n // bm, m // bn),
        in_specs=[pl.BlockSpec((bm, k), lambda i, j: (i, 0)), pl.BlockSpec((k, bn), lambda i, j: (0, j))],
        out_specs=pl.BlockSpec((bm, bn), lambda i, j: (i, j)),
        out_shape=jax.ShapeDtypeStruct((n, m), F32),
        compiler_params=_cparams(("parallel", "parallel")),
        name="matmul",
    )(x, w_bf16)


def _head_sum(x):
    n = x.shape[-1]
    same = (lax.broadcasted_iota(jnp.int32, (n, n), 0) // HEAD) == (lax.broadcasted_iota(jnp.int32, (n, n), 1) // HEAD)
    same = same.astype(BF16)
    return _dot(jnp.concatenate(_bf16_pieces(x, 3), axis=1), jnp.concatenate([same, same, same], axis=0))


def _rwkv_prep_body(p_ref, sh0_ref, mu_ref, w0_ref, wd2_ref, a0_ref, wa2_ref, wg2_ref, kk_ref, ka_ref, rk_ref,
                    r_o, lw_o, k_o, v_o, kn_o, bb_o, g_o, bonus_o, shift_o, carry, *, tb, t_valid, t_total):
    t = pl.program_id(1)

    @pl.when(t == 0)
    def _():
        carry[...] = sh0_ref[0]

    p = p_ref[...]
    row = lax.broadcasted_iota(jnp.int32, (tb, 1), 0)
    prev = jnp.where(row == 0, carry[...], pltpu.roll(p, 1, 0))
    carry[...] = p[tb - 1:tb, :]

    last = t_valid - 1
    @pl.when(t == last // tb)
    def _():
        shift_o[0] = p[last % tb:last % tb + 1, :]

    z = p + mu_ref[...] * (prev - p)
    r = z[:, 0:W_A]
    k = z[:, W_A:2 * W_A]
    v = z[:, 2 * W_A:3 * W_A]
    c0 = 3 * W_A
    wl = z[:, c0:c0 + R_DECAY]
    al = z[:, c0 + R_DECAY:c0 + R_DECAY + R_AAA]
    gl = z[:, c0 + R_DECAY + R_AAA:N_SHIFT]

    w_log = -_softplus(-(w0_ref[...] + _dot(jnp.tanh(wl).astype(BF16), wd2_ref[...]))) - 0.5
    lw = -jnp.exp(w_log)
    a = _sigmoid(a0_ref[...] + _dot(al.astype(BF16), wa2_ref[...]))
    g = _dot(_sigmoid(gl).astype(BF16), wg2_ref[...])
    kn = k * kk_ref[...]
    kn = kn / jnp.maximum(jnp.sqrt(_head_sum(kn * kn)), NORM_EPS)
    k2 = k * (1.0 + (a - 1.0) * ka_ref[...])
    bonus = _head_sum(r * k2 * rk_ref[...]) * v
    bb = kn * a
    if t_valid < t_total:
        ok = (t * tb + row) < t_valid
        lw = jnp.where(ok, lw, 0.0)
        kn = jnp.where(ok, kn, 0.0)
        bb = jnp.where(ok, bb, 0.0)
        k2 = jnp.where(ok, k2, 0.0)
        v = jnp.where(ok, v, 0.0)
    r_o[...] = r
    lw_o[...] = lw
    k_o[...] = k2
    v_o[...] = v
    kn_o[...] = kn
    bb_o[...] = bb
    g_o[...] = g
    bonus_o[...] = bonus


def _rwkv_prep(p_shift, shift0, lw, *, n_seq, t_total, t_valid):
    tb = min(t_total, 256)
    nt = t_total // tb
    n = n_seq * t_total
    row_spec = lambda w: pl.BlockSpec((tb, w), lambda b, t: (b * nt + t, 0))
    par = lambda a: pl.BlockSpec(a.shape, lambda b, t: (0,) * a.ndim)
    params = (lw['mu_shift'], lw['w0_decay'], lw['w_decay2'], lw['a0'], lw['w_a2'], lw['w_g2'],
              lw['k_k'], lw['k_a'], lw['r_k'])
    outs = pl.pallas_call(
        functools.partial(_rwkv_prep_body, tb=tb, t_valid=t_valid, t_total=t_total),
        grid=(n_seq, nt),
        in_specs=[row_spec(N_SHIFT), pl.BlockSpec((1, 1, N_SHIFT), lambda b, t: (b, 0, 0))] + [par(a) for a in params],
        out_specs=[row_spec(W_A)] * 8 + [pl.BlockSpec((1, 1, N_SHIFT), lambda b, t: (b, 0, 0))],
        out_shape=[jax.ShapeDtypeStruct((n, W_A), F32)] * 8 + [jax.ShapeDtypeStruct((n_seq, 1, N_SHIFT), F32)],
        scratch_shapes=[pltpu.VMEM((1, N_SHIFT), F32)],
        compiler_params=_cparams(("parallel", "arbitrary")),
        name="rwkv_prep",
    )(p_shift, shift0.reshape(n_seq, 1, N_SHIFT), *params)
    return outs[:8], outs[8].reshape(n_seq, N_SHIFT)


_INV_BLOCK = 16


def _nilpotent_inverses(ns, steps):
    invs = [-n for n in ns]
    sqs = list(ns)
    for _ in range(steps - 1):
        sqs = [_dot_bf16(q, q) for q in sqs]
        invs = [i + q + _dot_bf16(i, q) for i, q in zip(invs, sqs)]
    return invs


def _unit_lower_inverses_minus_i(ns, c):
    blk = min(_INV_BLOCK, c)
    if blk == c:
        return _nilpotent_inverses(ns, int(np.log2(c)))
    ri = lax.broadcasted_iota(jnp.int32, (c, c), 0)
    ci = lax.broadcasted_iota(jnp.int32, (c, c), 1)
    on_diag = (ri // blk) == (ci // blk)
    eye = (ri == ci).astype(F32)
    d_invs = [d + eye for d in _nilpotent_inverses([jnp.where(on_diag, n, 0.0) for n in ns], int(np.log2(blk)))]
    ms = [_dot_bf16(d, jnp.where(on_diag, 0.0, n)) for d, n in zip(d_invs, ns)]
    m_invs = [m + eye for m in _nilpotent_inverses(ms, int(np.log2(c // blk)))]
    return [_dot_bf16(m, d) - eye for m, d in zip(m_invs, d_invs)]


_WKV_SEQS = 2


def _wkv_body(r_ref, lw_ref, k_ref, v_ref, kn_ref, bb_ref, g_ref, bonus_ref, s0_ref, gng_ref, gnb_ref,
              o_ref, sT_ref, state, o_scr, *, c, n_par):
    ci = pl.program_id(1)

    @pl.when(ci == 0)
    def _():
        state[...] = s0_ref[...]

    ri = lax.broadcasted_iota(jnp.int32, (c, c), 0)
    cj = lax.broadcasted_iota(jnp.int32, (c, c), 1)
    incl = ri >= cj
    strict = ri > cj

    kap_h, r_h, k_h, b_h, kd_h, bd_h, v_h, etot_h = ([] for _ in range(8))
    for q in range(n_par):
        lw = lw_ref[q]
        cs = _dot_mask_f32(incl, lw)
        tot = cs[c - 1:c, :]
        e_out = jnp.exp(-cs)
        e_end = jnp.exp(tot - cs)
        k = k_ref[q]
        bb = bb_ref[q]
        per_seq = (kn_ref[q] * jnp.exp(cs - lw), r_ref[q] * jnp.exp(cs), k * e_out, bb * e_out, k * e_end, bb * e_end,
                   v_ref[q], jnp.exp(tot))
        for h in range(N_HEAD):
            sl = slice(h * HEAD, (h + 1) * HEAD)
            for lst, x in zip((kap_h, r_h, k_h, b_h, kd_h, bd_h, v_h, etot_h), per_seq):
                lst.append(x[:, sl])

    kr = [jnp.concatenate([kp, r], axis=0) for kp, r in zip(kap_h, r_h)]
    g_k = [_dot_bf16(x, y, 1, 1) for x, y in zip(kr, k_h)]
    g_b = [_dot_bf16(x, y, 1, 1) for x, y in zip(kr, b_h)]
    a_kk = [jnp.where(strict, g[:c], 0.0) for g in g_k]
    a_kr = [jnp.where(incl, g[c:], 0.0) for g in g_k]
    a_bk = [jnp.where(strict, g[:c], 0.0) for g in g_b]
    a_br = [jnp.where(incl, g[c:], 0.0) for g in g_b]
    t_inv = _unit_lower_inverses_minus_i(a_bk, c)
    rhs0 = [_dot_bf16(a, x) for a, x in zip(a_kk, v_h)]
    u0 = [x + _dot_bf16(t, x) for t, x in zip(t_inv, rhs0)]
    w1 = [x + _dot_bf16(t, x) for t, x in zip(t_inv, kap_h)]
    where = [(q, h) for q in range(n_par) for h in range(N_HEAD)]
    s_h = [state[q, h] for q, h in where]
    su = [_dot_bf16(jnp.concatenate([w, r], axis=0), s, 1, 1) for w, r, s in zip(w1, r_h, s_h)]
    u = [x[:c] + y for x, y in zip(su, u0)]
    o = [x[c:] + _dot_bf16(a, y) - _dot_bf16(b, z) for x, a, y, b, z in zip(su, a_kr, v_h, a_br, u)]
    for i, (q, h) in enumerate(where):
        vu = jnp.concatenate([v_h[i], -u[i]], axis=0)
        kb = jnp.concatenate([kd_h[i], bd_h[i]], axis=0)
        state[q, h] = s_h[i] * etot_h[i] + _dot_bf16(vu, kb, 0, 0)
    for i, (q, h) in enumerate(where):
        mu = jnp.mean(o[i], axis=-1, keepdims=True)
        oc = o[i] - mu
        var = jnp.mean(oc * oc, axis=-1, keepdims=True)
        o_scr[q, :, h * HEAD:(h + 1) * HEAD] = oc * lax.rsqrt(var + GN_EPS)

    o_ref[...] = (o_scr[...] * gng_ref[...] + gnb_ref[...] + bonus_ref[...]) * g_ref[...]

    @pl.when(ci == pl.num_programs(1) - 1)
    def _():
        sT_ref[...] = state[...]


def _wkv(prep, state0, gn_g, gn_b, *, n_seq, t_total):
    c = min(t_total, 64)
    nc = t_total // c
    n_par = _WKV_SEQS
    rows = pl.BlockSpec((n_par, c, W_A), lambda b, i: (b, i, 0))
    st = pl.BlockSpec((n_par, N_HEAD, HEAD, HEAD), lambda b, i: (b, 0, 0, 0))
    par = pl.BlockSpec((1, W_A), lambda b, i: (0, 0))
    o, s_end = pl.pallas_call(
        functools.partial(_wkv_body, c=c, n_par=n_par),
        grid=(n_seq // n_par, nc),
        in_specs=[rows] * 8 + [st, par, par],
        out_specs=[rows, st],
        out_shape=[jax.ShapeDtypeStruct((n_seq, t_total, W_A), F32),
                   jax.ShapeDtypeStruct((n_seq, N_HEAD, HEAD, HEAD), F32)],
        scratch_shapes=[pltpu.VMEM((n_par, N_HEAD, HEAD, HEAD), F32), pltpu.VMEM((n_par, c, W_A), F32)],
        compiler_params=_cparams(("parallel", "arbitrary")),
        name="wkv",
    )(*[a.reshape(n_seq, t_total, W_A) for a in prep], state0, gn_g, gn_b)
    return o.reshape(n_seq * t_total, W_A), s_end


def _later_key_mask(n):
    j = lax.broadcasted_iota(jnp.int32, (n, n), 0)
    s = lax.broadcasted_iota(jnp.int32, (n, n), 1)
    return jnp.where(j > s, -1.0, 0.0).astype(BF16)


_SP_LINEAR = 20.0


def _sb_softplus(z):
    return jnp.where(z > _SP_LINEAR, z, jnp.log(1.0 + jnp.exp(jnp.minimum(z, _SP_LINEAR))))


def _sb_scores(z, carry, later, visible=None):
    sp = _sb_softplus(z)
    if visible is not None:
        sp = jnp.where(visible, sp, 0.0)
    tail = _dot(sp.astype(BF16), later)
    att = jnp.exp(z - sp + tail + carry)
    if visible is not None:
        att = jnp.where(visible, att, 0.0)
    return att, carry + (tail - sp)[:, 0:1]


_SB_TQ = 512
_SB_RQ = 512
_SB_KW = 256


def _sb_prompt_body(qi_ref, kj_ref, q_ref, k_ref, v_ref, bias_ref, o_ref, acc, carry, *, tq, rq, kw):
    s = pl.program_id(1)
    qi = qi_ref[s]
    kj = kj_ref[s]
    later = _later_key_mask(kw)

    @pl.when(kj == qi)
    def _():
        acc[...] = jnp.zeros_like(acc)
        carry[...] = jnp.zeros_like(carry)

    def sweep(diagonal):
        masks = {}

        def visible_mask(qs, ks):
            if (qs, ks) not in masks:
                row_id = qs * rq + lax.broadcasted_iota(jnp.int32, (rq, kw), 0)
                col_id = ks * kw + lax.broadcasted_iota(jnp.int32, (rq, kw), 1)
                masks[qs, ks] = col_id < row_id
            return masks[qs, ks]

        for h in range(N_HEAD):
            sl = slice(h * HEAD, (h + 1) * HEAD)
            bias = bias_ref[h]
            for qs in range(tq // rq):
                rows = slice(qs * rq, (qs + 1) * rq)
                q = q_ref[rows, sl]
                cr = carry[h, rows, :]
                a = acc[rows, sl]
                for ks in reversed(range(tq // kw)):
                    visible = None
                    if diagonal:
                        if ks * kw >= (qs + 1) * rq:
                            continue
                        if (ks + 1) * kw > qs * rq:
                            visible = visible_mask(qs, ks)
                    keys = slice(ks * kw, (ks + 1) * kw)
                    att, cr = _sb_scores(_dot_nt(q, k_ref[keys, sl]) + bias, cr, later, visible)
                    a = a + _dot(att.astype(BF16), v_ref[keys, sl])
                acc[rows, sl] = a
                carry[h, rows, :] = cr

    @pl.when(kj == qi)
    def _():
        sweep(True)

    @pl.when(kj != qi)
    def _():
        sweep(False)

    @pl.when(kj == 0)
    def _():
        o_ref[...] = acc[...]


def _sb_prompt(q, k, v, bias, *, n_seq, t_total):
    tq = min(_SB_TQ, t_total)
    rq = min(_SB_RQ, tq)
    kw = min(_SB_KW, tq)
    nq = t_total // tq
    pairs = [(i, j) for i in range(nq) for j in range(i, -1, -1)]
    qi = jnp.asarray([p[0] for p in pairs], jnp.int32)
    kj = jnp.asarray([p[1] for p in pairs], jnp.int32)
    n = n_seq * t_total
    return pl.pallas_call(
        functools.partial(_sb_prompt_body, tq=tq, rq=rq, kw=kw),
        grid_spec=pltpu.PrefetchScalarGridSpec(
            num_scalar_prefetch=2,
            grid=(n_seq, len(pairs)),
            in_specs=[pl.BlockSpec((tq, W_A), lambda b, s, qi, kj: (b * nq + qi[s], 0)),
                      pl.BlockSpec((tq, W_A), lambda b, s, qi, kj: (b * nq + kj[s], 0)),
                      pl.BlockSpec((tq, W_A), lambda b, s, qi, kj: (b * nq + kj[s], 0)),
                      pl.BlockSpec(memory_space=pltpu.SMEM)],
            out_specs=pl.BlockSpec((tq, W_A), lambda b, s, qi, kj: (b * nq + qi[s], 0)),
            scratch_shapes=[pltpu.VMEM((tq, W_A), F32), pltpu.VMEM((N_HEAD, tq, 1), F32)]),
        out_shape=jax.ShapeDtypeStruct((n, W_A), F32),
        compiler_params=_cparams(("parallel", "arbitrary")),
        name="sb_prompt",
    )(qi, kj, q, k, v, bias)


_SB_PAGES = 8


def _sb_sample_body(pt_ref, q_ref, kn_ref, vn_ref, bias_ref, *refs):
    k_refs = refs[:_SB_PAGES]
    v_refs = refs[_SB_PAGES:2 * _SB_PAGES]
    o_ref, qx, bias_col, acc, carry = refs[2 * _SB_PAGES:]
    j = pl.program_id(1)
    m = N_HEAD * T_PAD
    later = _later_key_mask(PAGE)

    def pages(kv_refs, visible=None):
        zs = [_dot(qx[...], k_ref[0].astype(BF16)) + bias_col[...] for k_ref, _ in kv_refs]
        sps = [_sb_softplus(z) for z in zs]
        if visible is not None:
            sps = [jnp.where(visible, sp, 0.0) for sp in sps]
        tails = [_dot(sp.astype(BF16), later) for sp in sps]
        cr = carry[...]
        a = acc[...]
        for z, sp, tail, (_, v_ref) in zip(zs, sps, tails, kv_refs):
            att = jnp.exp(z - sp + tail + cr)
            if visible is not None:
                att = jnp.where(visible, att, 0.0)
            a = a + _dot_nt(att.astype(BF16), v_ref[0].astype(BF16))
            cr = cr + (tail - sp)[:, 0:1]
        carry[...] = cr
        acc[...] = a

    @pl.when(j == 0)
    def _():
        row_head = lax.broadcasted_iota(jnp.int32, (m, 1), 0) // T_PAD
        col_head = lax.broadcasted_iota(jnp.int32, (m, W_A), 1) // HEAD
        qx[...] = jnp.where(col_head == row_head, jnp.concatenate([q_ref[0]] * N_HEAD, axis=0), 0.0).astype(BF16)
        b = jnp.zeros((m, 1), F32)
        for h in range(N_HEAD):
            b = jnp.where(row_head == h, bias_ref[h], b)
        bias_col[...] = b
        acc[...] = jnp.zeros_like(acc)
        carry[...] = jnp.zeros_like(carry)
        tok = lax.broadcasted_iota(jnp.int32, (m, PAGE), 0) % T_PAD
        key = lax.broadcasted_iota(jnp.int32, (m, PAGE), 1)
        pages([(kn_ref, vn_ref)], key < tok)

    pages(list(zip(k_refs, v_refs)))

    @pl.when(j == pl.num_programs(1) - 1)
    def _():
        a = acc[...]
        col_head = lax.broadcasted_iota(jnp.int32, (T_PAD, W_A), 1) // HEAD
        o = jnp.zeros((T_PAD, W_A), F32)
        for h in range(N_HEAD):
            o = o + jnp.where(col_head == h, a[h * T_PAD:(h + 1) * T_PAD, :], 0.0)
        o_ref[0] = o


def _sb_sample(q, k_new, v_new, bias, k_cache, v_cache, page_table, layer):
    n_seq, n_pages = page_table.shape
    depth, n_pool = k_cache.shape[:2]
    as_pages = lambda c: c.transpose(0, 1, 3, 4, 2).reshape(depth * n_pool, W_A, PAGE)

    def new_page(x):
        x = x.reshape(n_seq, T_PAD, W_A).transpose(0, 2, 1)
        return jnp.pad(x, ((0, 0), (0, 0), (0, PAGE - T_PAD)))

    page_block = (1, W_A, PAGE)
    seq_page = pl.BlockSpec(page_block, lambda b, j, pt: (b, 0, 0))

    def page_spec(r):
        return pl.BlockSpec(page_block,
                            lambda b, j, pt: (layer * n_pool + pt[b * n_pages + n_pages - 1 - (j * _SB_PAGES + r)], 0, 0))

    m = N_HEAD * T_PAD
    out = pl.pallas_call(
        _sb_sample_body,
        grid_spec=pltpu.PrefetchScalarGridSpec(
            num_scalar_prefetch=1,
            grid=(n_seq, n_pages // _SB_PAGES),
            in_specs=[pl.BlockSpec((1, T_PAD, W_A), lambda b, j, pt: (b, 0, 0)), seq_page, seq_page,
                      pl.BlockSpec(memory_space=pltpu.SMEM)] + [page_spec(r) for r in range(_SB_PAGES)] * 2,
            out_specs=pl.BlockSpec((1, T_PAD, W_A), lambda b, j, pt: (b, 0, 0)),
            scratch_shapes=[pltpu.VMEM((m, W_A), BF16), pltpu.VMEM((m, 1), F32),
                            pltpu.VMEM((m, W_A), F32), pltpu.VMEM((m, 1), F32)]),
        out_shape=jax.ShapeDtypeStruct((n_seq, T_PAD, W_A), F32),
        compiler_params=_cparams(("parallel", "arbitrary")),
        name="sb_sample",
    )(page_table.reshape(-1), q.reshape(n_seq, T_PAD, W_A), new_page(k_new), new_page(v_new), bias,
      *([as_pages(k_cache)] * _SB_PAGES), *([as_pages(v_cache)] * _SB_PAGES))
    return out.reshape(n_seq * T_PAD, W_A)


def _mem_attn_body(q_ref, k_ref, v_ref, o_ref):
    for h in range(H_C):
        sl = slice(h * HD_C, (h + 1) * HD_C)
        s = _dot_nt(q_ref[:, sl].astype(BF16), k_ref[0, :, sl].astype(BF16)) * (HD_C ** -0.5)
        e = jnp.exp(s - jnp.max(s, axis=-1, keepdims=True))
        prob = e / jnp.sum(e, axis=-1, keepdims=True)
        o_ref[:, sl] = _dot(prob.astype(BF16), v_ref[0, :, sl].astype(BF16))


def _mem_attn(q, mem_k, mem_v, *, n_seq, t_total):
    tq = min(t_total, 512)
    nq = t_total // tq
    mem = pl.BlockSpec((1, N_MEM, W_C), lambda b, i: (b, 0, 0))
    row = pl.BlockSpec((tq, W_C), lambda b, i: (b * nq + i, 0))
    return pl.pallas_call(
        _mem_attn_body,
        grid=(n_seq, nq),
        in_specs=[row, mem, mem],
        out_specs=row,
        out_shape=jax.ShapeDtypeStruct((n_seq * t_total, W_C), F32),
        compiler_params=_cparams(("parallel", "parallel")),
        name="mem_attn",
    )(q, mem_k, mem_v)


def _combine_body(x_ref, oa_ref, ob_ref, oc_ref, gate_ref, wa_ref, wb_ref, wc_ref, wo_ref, g_ref, b_ref, y_ref):
    h = None
    for i, (o_ref, w_ref) in enumerate(((oa_ref, wa_ref), (ob_ref, wb_ref), (oc_ref, wc_ref))):
        part = _sigmoid(gate_ref[:, i * D_MODEL:(i + 1) * D_MODEL]) * _dot(o_ref[...].astype(BF16), w_ref[...])
        h = part if h is None else h + part
    y = DN_ALPHA * x_ref[...] + _dot(h.astype(BF16), wo_ref[...])
    y_ref[...] = _layer_norm(y, g_ref[...], b_ref[...])


def _combine(x, o_a, o_b, o_c, gate, wa, wb, wc, wo, ln_g, ln_b):
    n = x.shape[0]
    bm = min(n, 256)
    row = lambda w: pl.BlockSpec((bm, w), lambda i: (i, 0))
    const = lambda a: pl.BlockSpec(a.shape, lambda i: (0,) * a.ndim, pipeline_mode=pl.Buffered(1))
    return pl.pallas_call(
        _combine_body,
        grid=(n // bm,),
        in_specs=[row(D_MODEL), row(W_A), row(W_A), row(W_C), row(N_GATE)] + [const(a) for a in (wa, wb, wc, wo, ln_g, ln_b)],
        out_specs=row(D_MODEL),
        out_shape=jax.ShapeDtypeStruct((n, D_MODEL), F32),
        compiler_params=_cparams(("parallel",)),
        name="combine",
    )(x, o_a, o_b, o_c, gate, wa, wb, wc, wo, ln_g, ln_b)


def _top2(scores, ids, lowest):
    m1 = jnp.max(scores, axis=0, keepdims=True)
    i1 = jnp.min(jnp.where(scores == m1, ids, N_EXPERTS), axis=0, keepdims=True)
    rest = jnp.where(ids == i1, lowest, scores)
    m2 = jnp.max(rest, axis=0, keepdims=True)
    i2 = jnp.min(jnp.where(rest == m2, ids, N_EXPERTS), axis=0, keepdims=True)
    return m1, i1, m2, i2


def _router_gates(x, wr_t, br):
    m = x.shape[0]
    logits = _dot_f32(wr_t, x, 1, 1) + br
    e = jnp.exp(logits - jnp.max(logits, axis=0, keepdims=True))
    scores = e / jnp.sum(e, axis=0, keepdims=True)
    ids = lax.broadcasted_iota(jnp.int32, (N_EXPERTS, m), 0)
    grp = ids // E_PER_GROUP
    best = None
    for g in range(N_GROUPS):
        m1, _, m2, _ = _top2(jnp.where(grp == g, scores, -1.0), ids, -2.0)
        gs = m1 + m2
        if best is None:
            best, sel = gs, jnp.zeros((1, m), jnp.int32)
        else:
            sel = jnp.where(gs > best, g, sel)
            best = jnp.maximum(gs, best)
    m1, i1, m2, i2 = _top2(jnp.where(grp == sel, scores, -1.0), ids, -2.0)
    tot = m1 + m2
    return jnp.where(ids == i1, m1 / tot, 0.0) + jnp.where(ids == i2, m2 / tot, 0.0)


def _moe_body(x_ref, wr_ref, br_ref, w1_ref, w3_ref, w2_ref, g_ref, b_ref, y_ref, gate_scr, acc):
    e = pl.program_id(1)
    x = x_ref[...]

    @pl.when(e == 0)
    def _():
        gate_scr[...] = _router_gates(x, wr_ref[...], br_ref[...]).T
        acc[...] = jnp.zeros_like(acc)

    lane = lax.broadcasted_iota(jnp.int32, gate_scr.shape, 1)
    ge = jnp.sum(jnp.where(lane == e, gate_scr[...], 0.0), axis=1, keepdims=True)
    xb = x.astype(BF16)
    u = _dot(xb, w1_ref[0])
    h = (u * _sigmoid(u)) * _dot(xb, w3_ref[0])
    acc[...] += _dot((h * ge).astype(BF16), w2_ref[0])

    @pl.when(e == pl.num_programs(1) - 1)
    def _():
        y_ref[...] = _layer_norm(DN_ALPHA * x + acc[...], g_ref[...], b_ref[...])


def _moe(x, wr_t, br, w1, w3, w2, ln_g, ln_b):
    n = x.shape[0]
    bm = min(n, 1024)
    const = lambda a: pl.BlockSpec(a.shape, lambda i, e: (0,) * a.ndim)
    return pl.pallas_call(
        _moe_body,
        grid=(n // bm, N_EXPERTS),
        in_specs=[pl.BlockSpec((bm, D_MODEL), lambda i, e: (i, 0)), const(wr_t), const(br),
                  pl.BlockSpec((1, D_MODEL, D_FF), lambda i, e: (e, 0, 0)),
                  pl.BlockSpec((1, D_MODEL, D_FF), lambda i, e: (e, 0, 0)),
                  pl.BlockSpec((1, D_FF, D_MODEL), lambda i, e: (e, 0, 0)),
                  const(ln_g), const(ln_b)],
        out_specs=pl.BlockSpec((bm, D_MODEL), lambda i, e: (i, 0)),
        out_shape=jax.ShapeDtypeStruct((n, D_MODEL), F32),
        scratch_shapes=[pltpu.VMEM((bm, N_EXPERTS), F32), pltpu.VMEM((bm, D_MODEL), F32)],
        compiler_params=_cparams(("parallel", "arbitrary")),
        name="moe",
    )(x, wr_t, br, w1, w3, w2, ln_g, ln_b)


def _trunk_layer(x, lw, shared, sb_fn, mem_k, mem_v, wkv0, shift0, *, n_seq, t_total, t_valid, kv_channel_major):
    p_shift, q16, k_b, k16, v_b, v16, q_c, gate = _in_proj(x, lw['w_in'], (n_seq, t_total) if kv_channel_major else None)
    prep, shift = _rwkv_prep(p_shift, shift0, lw, n_seq=n_seq, t_total=t_total, t_valid=t_valid)
    o_a, wkv = _wkv(prep, wkv0, lw['gn_g'], lw['gn_b'], n_seq=n_seq, t_total=t_total)
    o_b = sb_fn(q16, k_b, k16, v_b, v16)
    o_c = _mem_attn(q_c, mem_k, mem_v, n_seq=n_seq, t_total=t_total)
    x = _combine(x, o_a, o_b, o_c, gate, lw['w_br_a'], lw['w_br_b'], lw['w_br_c'], lw['w_out'], lw['ln1_g'], lw['ln1_b'])
    x = _moe(x, shared['wr_t'], shared['br'], lw['w1'], lw['w3'], lw['w2'], lw['ln2_g'], lw['ln2_b'])
    return x, k_b, v_b, wkv, shift


def kernel(x_prompt, x_sample, mem_prompt, cache_sb_k, cache_sb_v, cache_mem_k, cache_mem_v, state_wkv, state_shift, page_table, w_in, mu_shift, w0_decay, w_decay2, a0, w_a2, w_g2, k_k, k_a, r_k, gn_g, gn_b, sb_bias, w_mem_kv, w_br_a, w_br_b, w_br_c, w_out, ln1_g, ln1_b, w_router, b_router, w1, w3, w2, ln2_g, ln2_b):
    depth = w_in.shape[0]
    n_p, t_p = x_prompt.shape[:2]
    n_s, t_s = x_sample.shape[:2]
    n_pool = cache_sb_k.shape[1]
    row = lambda a, l: a[l].reshape(1, -1)
    shared = {'wr_t': w_router.T, 'br': b_router.reshape(-1, 1)}

    xp = x_prompt.reshape(n_p * t_p, D_MODEL)
    xs = jnp.pad(x_sample, ((0, 0), (0, T_PAD - t_s), (0, 0))).reshape(n_s * T_PAD, D_MODEL)
    mem_rows = mem_prompt.reshape(n_p * N_MEM, D_MODEL)
    zero_wkv = jnp.zeros((n_p, N_HEAD, HEAD, HEAD), F32)
    zero_shift = jnp.zeros((n_p, N_SHIFT), F32)

    outs = [[] for _ in range(10)]
    for l in range(depth):
        lw = {'w_in': w_in[l].astype(BF16), 'mu_shift': row(mu_shift, l), 'w0_decay': row(w0_decay, l),
              'w_decay2': w_decay2[l].astype(BF16), 'a0': row(a0, l), 'w_a2': w_a2[l].astype(BF16),
              'w_g2': w_g2[l].astype(BF16), 'k_k': row(k_k, l), 'k_a': row(k_a, l), 'r_k': row(r_k, l),
              'gn_g': row(gn_g, l), 'gn_b': row(gn_b, l),
              'w_br_a': w_br_a[l].astype(BF16), 'w_br_b': w_br_b[l].astype(BF16), 'w_br_c': w_br_c[l].astype(BF16),
              'w_out': w_out[l].astype(BF16), 'ln1_g': row(ln1_g, l), 'ln1_b': row(ln1_b, l),
              'w1': w1[l].astype(BF16), 'w3': w3[l].astype(BF16), 'w2': w2[l].astype(BF16),
              'ln2_g': row(ln2_g, l), 'ln2_b': row(ln2_b, l)}
        bias = sb_bias[l]

        mem_kv = _matmul(mem_rows, w_mem_kv[l].astype(BF16))
        mk = mem_kv[:, :W_C].reshape(n_p, N_MEM, W_C)
        mv = mem_kv[:, W_C:].reshape(n_p, N_MEM, W_C)
        sb_p = functools.partial(_sb_prompt, bias=bias, n_seq=n_p, t_total=t_p)
        xp, kb, vb, wkv, sh = _trunk_layer(xp, lw, shared, lambda q16, k, k16, v, v16: sb_p(q16, k16, v16), mk, mv,
                                           zero_wkv, zero_shift, n_seq=n_p, t_total=t_p, t_valid=t_p, kv_channel_major=True)
        for i, a in zip((0, 1, 4, 5, 6, 8), (kb.reshape(n_p, N_HEAD, HEAD, t_p), vb.reshape(n_p, N_HEAD, HEAD, t_p),
                                             mk.reshape(n_p, N_MEM, H_C, HD_C), mv.reshape(n_p, N_MEM, H_C, HD_C), wkv, sh)):
            outs[i].append(a)

        def sb_s(q16, k, k16, v, v16):
            return _sb_sample(q16.astype(F32), k, v, bias, cache_sb_k, cache_sb_v, page_table, l)

        xs, kb, vb, wkv, sh = _trunk_layer(xs, lw, shared, sb_s, cache_mem_k[l].reshape(n_s, N_MEM, W_C),
                                           cache_mem_v[l].reshape(n_s, N_MEM, W_C), state_wkv[l], state_shift[l],
                                           n_seq=n_s, t_total=T_PAD, t_valid=t_s, kv_channel_major=False)
        for i, a in zip((2, 3, 7, 9), (kb.reshape(n_s, T_PAD, N_HEAD, HEAD)[:, :t_s], vb.reshape(n_s, T_PAD, N_HEAD, HEAD)[:, :t_s],
                                       wkv, sh)):
            outs[i].append(a)

    st = [jnp.stack(o) for o in outs]
    for i in (0, 1):
        st[i] = st[i].transpose(0, 1, 4, 2, 3)
    y_p = xp.reshape(n_p, t_p, D_MODEL)
    y_s = xs.reshape(n_s, T_PAD, D_MODEL)[:, :t_s]
    return (y_p, y_s, st[0], st[1], st[2], st[3], st[4], st[5], st[6], st[7], st[8], st[9])
```

```python
import functools

import numpy as np
import jax
import jax.numpy as jnp
from jax import lax
from jax.experimental import pallas as pl
from jax.experimental.pallas import tpu as pltpu

F32 = jnp.float32
BF16 = jnp.bfloat16

D_MODEL = 1024
DEPTH = 4
N_HEAD = 8
HEAD = 64
W_A = N_HEAD * HEAD
R_DECAY, R_AAA, R_GATE = 64, 64, 128
H_C, HD_C = 4, 128
W_C = H_C * HD_C
N_MEM = 256
N_SHIFT = 3 * W_A + R_DECAY + R_AAA + R_GATE
N_GATE = 3 * D_MODEL
N_IN = N_SHIFT + 3 * W_A + W_C + N_GATE
N_EXPERTS, N_GROUPS, E_PER_GROUP, D_FF = 16, 4, 4, 512
DN_ALPHA = (2 * DEPTH) ** 0.25
LN_EPS = 1e-5
GN_EPS = 64e-5
NORM_EPS = 1e-12
PAGE = 128
LANE = 128
SUBLANE = 8
T_PAD = SUBLANE
VMEM_LIMIT = 56 * 1024 * 1024


def _cparams(sem):
    return pltpu.CompilerParams(dimension_semantics=sem, vmem_limit_bytes=VMEM_LIMIT)


def _softplus(z):
    return jnp.maximum(z, 0.0) + jnp.log1p(jnp.exp(-jnp.abs(z)))


def _sigmoid(z):
    return 1.0 / (1.0 + jnp.exp(-z))


def _dot(a, b):
    return jnp.dot(a, b, preferred_element_type=F32)


def _dot_nt(a, b):
    return lax.dot_general(a, b, (((1,), (1,)), ((), ())), preferred_element_type=F32)


def _bf16_pieces(a, n):
    pieces = []
    for _ in range(n):
        p = a.astype(BF16)
        pieces.append(p)
        a = a - p.astype(F32)
    return pieces


def _dot_f32(a, b, ca=1, cb=0):
    ah, al = _bf16_pieces(a, 2)
    bh, bl = _bf16_pieces(b, 2)
    return lax.dot_general(jnp.concatenate([ah, ah, al], axis=ca), jnp.concatenate([bh, bl, bh], axis=cb),
                           (((ca,), (cb,)), ((), ())), preferred_element_type=F32)


def _dot_bf16(a, b, ca=1, cb=0):
    return lax.dot_general(a.astype(BF16), b.astype(BF16), (((ca,), (cb,)), ((), ())), preferred_element_type=F32)


def _dot_mask_f32(mask, b):
    m = mask.astype(BF16)
    return _dot(jnp.concatenate([m, m, m], axis=1), jnp.concatenate(_bf16_pieces(b, 3), axis=0))


def _layer_norm(y, g, b):
    mu = jnp.mean(y, axis=-1, keepdims=True)
    yc = y - mu
    var = jnp.mean(yc * yc, axis=-1, keepdims=True)
    return yc * lax.rsqrt(var + LN_EPS) * g + b


_MM_COLS = 256
_SB_SCALE = HEAD ** -0.5
_IN_GROUPS = ((N_SHIFT, True, False, 1.0),
              (W_A, False, True, _SB_SCALE),
              (W_A, True, True, 1.0),
              (W_A, True, True, 1.0),
              (W_C, True, False, 1.0),
              (N_GATE, True, False, 1.0))


_IN_KV_GROUPS = (2, 3)


def _in_proj_body(x_ref, w_ref, *out_refs, kv_transposed):
    x = x_ref[...].astype(BF16)
    refs = iter(out_refs)
    off = 0
    for gi, (width, want_f32, want_bf16, scale) in enumerate(_IN_GROUPS):
        ref_f32 = next(refs) if want_f32 else None
        ref_bf16 = next(refs) if want_bf16 else None
        for c in range(0, width, _MM_COLS):
            cols = slice(c, c + _MM_COLS)
            y = _dot(x, w_ref[:, off + c:off + c + _MM_COLS])
            if want_f32 and kv_transposed and gi in _IN_KV_GROUPS:
                ref_f32[cols, :] = y.T
            elif want_f32:
                ref_f32[:, cols] = y
            if want_bf16:
                ref_bf16[:, cols] = (y * scale).astype(BF16)
        off += width


def _in_proj(x, w_in_bf16, kv_seq=None):
    n = x.shape[0]
    bm = min(n, 256)
    row_spec = lambda w: pl.BlockSpec((bm, w), lambda i: (i, 0))
    specs, shapes = [], []
    for gi, (width, want_f32, want_bf16, _) in enumerate(_IN_GROUPS):
        if want_f32 and kv_seq is not None and gi in _IN_KV_GROUPS:
            n_seq, t_total = kv_seq
            nt = t_total // bm
            specs.append(pl.BlockSpec((width, bm), lambda i: (i // nt, i % nt)))
            shapes.append(jax.ShapeDtypeStruct((n_seq * width, t_total), F32))
        elif want_f32:
            specs.append(row_spec(width))
            shapes.append(jax.ShapeDtypeStruct((n, width), F32))
        if want_bf16:
            specs.append(row_spec(width))
            shapes.append(jax.ShapeDtypeStruct((n, width), BF16))
    return pl.pallas_call(
        functools.partial(_in_proj_body, kv_transposed=kv_seq is not None),
        grid=(n // bm,),
        in_specs=[pl.BlockSpec((bm, D_MODEL), lambda i: (i, 0)),
                  pl.BlockSpec((D_MODEL, N_IN), lambda i: (0, 0), pipeline_mode=pl.Buffered(1))],
        out_specs=specs,
        out_shape=shapes,
        compiler_params=_cparams(("parallel",)),
        name="in_proj",
    )(x, w_in_bf16)


def _matmul_body(x_ref, w_ref, o_ref):
    o_ref[...] = _dot(x_ref[...].astype(BF16), w_ref[...])


def _matmul(x, w_bf16, bn=512):
    n, k = x.shape
    m = w_bf16.shape[1]
    bm = min(n, 256)
    return pl.pallas_call(
        _matmul_body,
        grid=(n // bm, m // bn),
        in_specs=[pl.BlockSpec((bm, k), lambda i, j: (i, 0)), pl.BlockSpec((k, bn), lambda i, j: (0, j))],
        out_specs=pl.BlockSpec((bm, bn), lambda i, j: (i, j)),
        out_shape=jax.ShapeDtypeStruct((n, m), F32),
        compiler_params=_cparams(("parallel", "parallel")),
        name="matmul",
    )(x, w_bf16)


def _head_sum(x):
    n = x.shape[-1]
    same = (lax.broadcasted_iota(jnp.int32, (n, n), 0) // HEAD) == (lax.broadcasted_iota(jnp.int32, (n, n), 1) // HEAD)
    same = same.astype(BF16)
    return _dot(jnp.concatenate(_bf16_pieces(x, 3), axis=1), jnp.concatenate([same, same, same], axis=0))


def _rwkv_prep_body(p_ref, sh0_ref, mu_ref, w0_ref, wd2_ref, a0_ref, wa2_ref, wg2_ref, kk_ref, ka_ref, rk_ref,
                    r_o, lw_o, k_o, v_o, kn_o, bb_o, g_o, bonus_o, shift_o, carry, *, tb, t_valid, t_total):
    t = pl.program_id(1)

    @pl.when(t == 0)
    def _():
        carry[...] = sh0_ref[0]

    p = p_ref[...]
    row = lax.broadcasted_iota(jnp.int32, (tb, 1), 0)
    prev = jnp.where(row == 0, carry[...], pltpu.roll(p, 1, 0))
    carry[...] = p[tb - 1:tb, :]

    last = t_valid - 1
    @pl.when(t == last // tb)
    def _():
        shift_o[0] = p[last % tb:last % tb + 1, :]

    z = p + mu_ref[...] * (prev - p)
    r = z[:, 0:W_A]
    k = z[:, W_A:2 * W_A]
    v = z[:, 2 * W_A:3 * W_A]
    c0 = 3 * W_A
    wl = z[:, c0:c0 + R_DECAY]
    al = z[:, c0 + R_DECAY:c0 + R_DECAY + R_AAA]
    gl = z[:, c0 + R_DECAY + R_AAA:N_SHIFT]

    w_log = -_softplus(-(w0_ref[...] + _dot(jnp.tanh(wl).astype(BF16), wd2_ref[...]))) - 0.5
    lw = -jnp.exp(w_log)
    a = _sigmoid(a0_ref[...] + _dot(al.astype(BF16), wa2_ref[...]))
    g = _dot(_sigmoid(gl).astype(BF16), wg2_ref[...])
    kn = k * kk_ref[...]
    kn = kn / jnp.maximum(jnp.sqrt(_head_sum(kn * kn)), NORM_EPS)
    k2 = k * (1.0 + (a - 1.0) * ka_ref[...])
    bonus = _head_sum(r * k2 * rk_ref[...]) * v
    bb = kn * a
    if t_valid < t_total:
        ok = (t * tb + row) < t_valid
        lw = jnp.where(ok, lw, 0.0)
        kn = jnp.where(ok, kn, 0.0)
        bb = jnp.where(ok, bb, 0.0)
        k2 = jnp.where(ok, k2, 0.0)
        v = jnp.where(ok, v, 0.0)
    r_o[...] = r
    lw_o[...] = lw
    k_o[...] = k2
    v_o[...] = v
    kn_o[...] = kn
    bb_o[...] = bb
    g_o[...] = g
    bonus_o[...] = bonus


def _rwkv_prep(p_shift, shift0, lw, *, n_seq, t_total, t_valid):
    tb = min(t_total, 256)
    nt = t_total // tb
    n = n_seq * t_total
    row_spec = lambda w: pl.BlockSpec((tb, w), lambda b, t: (b * nt + t, 0))
    par = lambda a: pl.BlockSpec(a.shape, lambda b, t: (0,) * a.ndim)
    params = (lw['mu_shift'], lw['w0_decay'], lw['w_decay2'], lw['a0'], lw['w_a2'], lw['w_g2'],
              lw['k_k'], lw['k_a'], lw['r_k'])
    outs = pl.pallas_call(
        functools.partial(_rwkv_prep_body, tb=tb, t_valid=t_valid, t_total=t_total),
        grid=(n_seq, nt),
        in_specs=[row_spec(N_SHIFT), pl.BlockSpec((1, 1, N_SHIFT), lambda b, t: (b, 0, 0))] + [par(a) for a in params],
        out_specs=[row_spec(W_A)] * 8 + [pl.BlockSpec((1, 1, N_SHIFT), lambda b, t: (b, 0, 0))],
        out_shape=[jax.ShapeDtypeStruct((n, W_A), F32)] * 8 + [jax.ShapeDtypeStruct((n_seq, 1, N_SHIFT), F32)],
        scratch_shapes=[pltpu.VMEM((1, N_SHIFT), F32)],
        compiler_params=_cparams(("parallel", "arbitrary")),
        name="rwkv_prep",
    )(p_shift, shift0.reshape(n_seq, 1, N_SHIFT), *params)
    return outs[:8], outs[8].reshape(n_seq, N_SHIFT)


_INV_BLOCK = 16


def _nilpotent_inverses(ns, steps):
    invs = [-n for n in ns]
    sqs = list(ns)
    for _ in range(steps - 1):
        sqs = [_dot_bf16(q, q) for q in sqs]
        invs = [i + q + _dot_bf16(i, q) for i, q in zip(invs, sqs)]
    return invs


def _unit_lower_inverses_minus_i(ns, c):
    blk = min(_INV_BLOCK, c)
    if blk == c:
        return _nilpotent_inverses(ns, int(np.log2(c)))
    ri = lax.broadcasted_iota(jnp.int32, (c, c), 0)
    ci = lax.broadcasted_iota(jnp.int32, (c, c), 1)
    on_diag = (ri // blk) == (ci // blk)
    eye = (ri == ci).astype(F32)
    d_invs = [d + eye for d in _nilpotent_inverses([jnp.where(on_diag, n, 0.0) for n in ns], int(np.log2(blk)))]
    ms = [_dot_bf16(d, jnp.where(on_diag, 0.0, n)) for d, n in zip(d_invs, ns)]
    m_invs = [m + eye for m in _nilpotent_inverses(ms, int(np.log2(c // blk)))]
    return [_dot_bf16(m, d) - eye for m, d in zip(m_invs, d_invs)]


_WKV_SEQS = 4


def _wkv_body(r_ref, lw_ref, k_ref, v_ref, kn_ref, bb_ref, g_ref, bonus_ref, s0_ref, gng_ref, gnb_ref,
              o_ref, sT_ref, state, o_scr, *, c, n_par):
    ci = pl.program_id(1)

    @pl.when(ci == 0)
    def _():
        state[...] = s0_ref[...]

    ri = lax.broadcasted_iota(jnp.int32, (c, c), 0)
    cj = lax.broadcasted_iota(jnp.int32, (c, c), 1)
    incl = ri >= cj
    strict = ri > cj

    kap_h, r_h, k_h, b_h, kd_h, bd_h, v_h, etot_h = ([] for _ in range(8))
    for q in range(n_par):
        lw = lw_ref[q]
        cs = _dot_mask_f32(incl, lw)
        tot = cs[c - 1:c, :]
        e_out = jnp.exp(-cs)
        e_end = jnp.exp(tot - cs)
        k = k_ref[q]
        bb = bb_ref[q]
        per_seq = (kn_ref[q] * jnp.exp(cs - lw), r_ref[q] * jnp.exp(cs), k * e_out, bb * e_out, k * e_end, bb * e_end,
                   v_ref[q], jnp.exp(tot))
        for h in range(N_HEAD):
            sl = slice(h * HEAD, (h + 1) * HEAD)
            for lst, x in zip((kap_h, r_h, k_h, b_h, kd_h, bd_h, v_h, etot_h), per_seq):
                lst.append(x[:, sl])

    kr = [jnp.concatenate([kp, r], axis=0) for kp, r in zip(kap_h, r_h)]
    g_k = [_dot_bf16(x, y, 1, 1) for x, y in zip(kr, k_h)]
    g_b = [_dot_bf16(x, y, 1, 1) for x, y in zip(kr, b_h)]
    a_kk = [jnp.where(strict, g[:c], 0.0) for g in g_k]
    a_kr = [jnp.where(incl, g[c:], 0.0) for g in g_k]
    a_bk = [jnp.where(strict, g[:c], 0.0) for g in g_b]
    a_br = [jnp.where(incl, g[c:], 0.0) for g in g_b]
    t_inv = _unit_lower_inverses_minus_i(a_bk, c)
    rhs0 = [_dot_bf16(a, x) for a, x in zip(a_kk, v_h)]
    u0 = [x + _dot_bf16(t, x) for t, x in zip(t_inv, rhs0)]
    w1 = [x + _dot_bf16(t, x) for t, x in zip(t_inv, kap_h)]
    where = [(q, h) for q in range(n_par) for h in range(N_HEAD)]
    s_h = [state[q, h] for q, h in where]
    su = [_dot_bf16(jnp.concatenate([w, r], axis=0), s, 1, 1) for w, r, s in zip(w1, r_h, s_h)]
    u = [x[:c] + y for x, y in zip(su, u0)]
    o = [x[c:] + _dot_bf16(a, y) - _dot_bf16(b, z) for x, a, y, b, z in zip(su, a_kr, v_h, a_br, u)]
    for i, (q, h) in enumerate(where):
        vu = jnp.concatenate([v_h[i], -u[i]], axis=0)
        kb = jnp.concatenate([kd_h[i], bd_h[i]], axis=0)
        state[q, h] = s_h[i] * etot_h[i] + _dot_bf16(vu, kb, 0, 0)
    for i, (q, h) in enumerate(where):
        mu = jnp.mean(o[i], axis=-1, keepdims=True)
        oc = o[i] - mu
        var = jnp.mean(oc * oc, axis=-1, keepdims=True)
        o_scr[q, :, h * HEAD:(h + 1) * HEAD] = oc * lax.rsqrt(var + GN_EPS)

    o_ref[...] = (o_scr[...] * gng_ref[...] + gnb_ref[...] + bonus_ref[...]) * g_ref[...]

    @pl.when(ci == pl.num_programs(1) - 1)
    def _():
        sT_ref[...] = state[...]


def _wkv(prep, state0, gn_g, gn_b, *, n_seq, t_total):
    c = min(t_total, 64)
    nc = t_total // c
    n_par = _WKV_SEQS
    rows = pl.BlockSpec((n_par, c, W_A), lambda b, i: (b, i, 0))
    st = pl.BlockSpec((n_par, N_HEAD, HEAD, HEAD), lambda b, i: (b, 0, 0, 0))
    par = pl.BlockSpec((1, W_A), lambda b, i: (0, 0))
    o, s_end = pl.pallas_call(
        functools.partial(_wkv_body, c=c, n_par=n_par),
        grid=(n_seq // n_par, nc),
        in_specs=[rows] * 8 + [st, par, par],
        out_specs=[rows, st],
        out_shape=[jax.ShapeDtypeStruct((n_seq, t_total, W_A), F32),
                   jax.ShapeDtypeStruct((n_seq, N_HEAD, HEAD, HEAD), F32)],
        scratch_shapes=[pltpu.VMEM((n_par, N_HEAD, HEAD, HEAD), F32), pltpu.VMEM((n_par, c, W_A), F32)],
        compiler_params=_cparams(("parallel", "arbitrary")),
        name="wkv",
    )(*[a.reshape(n_seq, t_total, W_A) for a in prep], state0, gn_g, gn_b)
    return o.reshape(n_seq * t_total, W_A), s_end


def _later_key_mask(n):
    j = lax.broadcasted_iota(jnp.int32, (n, n), 0)
    s = lax.broadcasted_iota(jnp.int32, (n, n), 1)
    return jnp.where(j > s, -1.0, 0.0).astype(BF16)


_SP_LINEAR = 20.0


def _sb_softplus(z):
    return jnp.where(z > _SP_LINEAR, z, jnp.log(1.0 + jnp.exp(jnp.minimum(z, _SP_LINEAR))))


def _sb_scores(z, carry, later, visible=None):
    sp = _sb_softplus(z)
    if visible is not None:
        sp = jnp.where(visible, sp, 0.0)
    tail = _dot(sp.astype(BF16), later)
    att = jnp.exp(z - sp + tail + carry)
    if visible is not None:
        att = jnp.where(visible, att, 0.0)
    return att, carry + (tail - sp)[:, 0:1]


_SB_TQ = 512
_SB_RQ = 512
_SB_KW = 256


def _sb_prompt_body(qi_ref, kj_ref, q_ref, k_ref, v_ref, bias_ref, o_ref, acc, carry, *, tq, rq, kw):
    s = pl.program_id(1)
    qi = qi_ref[s]
    kj = kj_ref[s]
    later = _later_key_mask(kw)

    @pl.when(kj == qi)
    def _():
        acc[...] = jnp.zeros_like(acc)
        carry[...] = jnp.zeros_like(carry)

    def sweep(diagonal):
        masks = {}

        def visible_mask(qs, ks):
            if (qs, ks) not in masks:
                row_id = qs * rq + lax.broadcasted_iota(jnp.int32, (rq, kw), 0)
                col_id = ks * kw + lax.broadcasted_iota(jnp.int32, (rq, kw), 1)
                masks[qs, ks] = col_id < row_id
            return masks[qs, ks]

        for h in range(N_HEAD):
            sl = slice(h * HEAD, (h + 1) * HEAD)
            bias = bias_ref[h]
            for qs in range(tq // rq):
                rows = slice(qs * rq, (qs + 1) * rq)
                q = q_ref[rows, sl]
                cr = carry[h, rows, :]
                a = acc[rows, sl]
                for ks in reversed(range(tq // kw)):
                    visible = None
                    if diagonal:
                        if ks * kw >= (qs + 1) * rq:
                            continue
                        if (ks + 1) * kw > qs * rq:
                            visible = visible_mask(qs, ks)
                    keys = slice(ks * kw, (ks + 1) * kw)
                    att, cr = _sb_scores(_dot_nt(q, k_ref[keys, sl]) + bias, cr, later, visible)
                    a = a + _dot(att.astype(BF16), v_ref[keys, sl])
                acc[rows, sl] = a
                carry[h, rows, :] = cr

    @pl.when(kj == qi)
    def _():
        sweep(True)

    @pl.when(kj != qi)
    def _():
        sweep(False)

    @pl.when(kj == 0)
    def _():
        o_ref[...] = acc[...]


def _sb_prompt(q, k, v, bias, *, n_seq, t_total):
    tq = min(_SB_TQ, t_total)
    rq = min(_SB_RQ, tq)
    kw = min(_SB_KW, tq)
    nq = t_total // tq
    pairs = [(i, j) for i in range(nq) for j in range(i, -1, -1)]
    qi = jnp.asarray([p[0] for p in pairs], jnp.int32)
    kj = jnp.asarray([p[1] for p in pairs], jnp.int32)
    n = n_seq * t_total
    return pl.pallas_call(
        functools.partial(_sb_prompt_body, tq=tq, rq=rq, kw=kw),
        grid_spec=pltpu.PrefetchScalarGridSpec(
            num_scalar_prefetch=2,
            grid=(n_seq, len(pairs)),
            in_specs=[pl.BlockSpec((tq, W_A), lambda b, s, qi, kj: (b * nq + qi[s], 0)),
                      pl.BlockSpec((tq, W_A), lambda b, s, qi, kj: (b * nq + kj[s], 0)),
                      pl.BlockSpec((tq, W_A), lambda b, s, qi, kj: (b * nq + kj[s], 0)),
                      pl.BlockSpec(memory_space=pltpu.SMEM)],
            out_specs=pl.BlockSpec((tq, W_A), lambda b, s, qi, kj: (b * nq + qi[s], 0)),
            scratch_shapes=[pltpu.VMEM((tq, W_A), F32), pltpu.VMEM((N_HEAD, tq, 1), F32)]),
        out_shape=jax.ShapeDtypeStruct((n, W_A), F32),
        compiler_params=_cparams(("parallel", "arbitrary")),
        name="sb_prompt",
    )(qi, kj, q, k, v, bias)


_SB_PAGES = 16


def _sb_sample_body(pt_ref, q_ref, kn_ref, vn_ref, bias_ref, *refs):
    k_refs = refs[:_SB_PAGES]
    v_refs = refs[_SB_PAGES:2 * _SB_PAGES]
    o_ref, qx, bias_col, acc, carry = refs[2 * _SB_PAGES:]
    j = pl.program_id(1)
    m = N_HEAD * T_PAD
    later = _later_key_mask(PAGE)

    def pages(kv_refs, visible=None):
        zs = [_dot(qx[...], k_ref[0].astype(BF16)) + bias_col[...] for k_ref, _ in kv_refs]
        sps = [_sb_softplus(z) for z in zs]
        if visible is not None:
            sps = [jnp.where(visible, sp, 0.0) for sp in sps]
        tails = [_dot(sp.astype(BF16), later) for sp in sps]
        cr = carry[...]
        a = acc[...]
        for z, sp, tail, (_, v_ref) in zip(zs, sps, tails, kv_refs):
            att = jnp.exp(z - sp + tail + cr)
            if visible is not None:
                att = jnp.where(visible, att, 0.0)
            a = a + _dot_nt(att.astype(BF16), v_ref[0].astype(BF16))
            cr = cr + (tail - sp)[:, 0:1]
        carry[...] = cr
        acc[...] = a

    @pl.when(j == 0)
    def _():
        row_head = lax.broadcasted_iota(jnp.int32, (m, 1), 0) // T_PAD
        col_head = lax.broadcasted_iota(jnp.int32, (m, W_A), 1) // HEAD
        qx[...] = jnp.where(col_head == row_head, jnp.concatenate([q_ref[0]] * N_HEAD, axis=0), 0.0).astype(BF16)
        b = jnp.zeros((m, 1), F32)
        for h in range(N_HEAD):
            b = jnp.where(row_head == h, bias_ref[h], b)
        bias_col[...] = b
        acc[...] = jnp.zeros_like(acc)
        carry[...] = jnp.zeros_like(carry)
        tok = lax.broadcasted_iota(jnp.int32, (m, PAGE), 0) % T_PAD
        key = lax.broadcasted_iota(jnp.int32, (m, PAGE), 1)
        pages([(kn_ref, vn_ref)], key < tok)

    pages(list(zip(k_refs, v_refs)))

    @pl.when(j == pl.num_programs(1) - 1)
    def _():
        a = acc[...]
        col_head = lax.broadcasted_iota(jnp.int32, (T_PAD, W_A), 1) // HEAD
        o = jnp.zeros((T_PAD, W_A), F32)
        for h in range(N_HEAD):
            o = o + jnp.where(col_head == h, a[h * T_PAD:(h + 1) * T_PAD, :], 0.0)
        o_ref[0] = o


def _sb_sample(q, k_new, v_new, bias, k_cache, v_cache, page_table, layer):
    n_seq, n_pages = page_table.shape
    depth, n_pool = k_cache.shape[:2]
    as_pages = lambda c: c.transpose(0, 1, 3, 4, 2).reshape(depth * n_pool, W_A, PAGE)

    def new_page(x):
        x = x.reshape(n_seq, T_PAD, W_A).transpose(0, 2, 1)
        return jnp.pad(x, ((0, 0), (0, 0), (0, PAGE - T_PAD)))

    page_block = (1, W_A, PAGE)
    seq_page = pl.BlockSpec(page_block, lambda b, j, pt: (b, 0, 0))

    def page_spec(r):
        return pl.BlockSpec(page_block,
                            lambda b, j, pt: (layer * n_pool + pt[b * n_pages + n_pages - 1 - (j * _SB_PAGES + r)], 0, 0))

    m = N_HEAD * T_PAD
    out = pl.pallas_call(
        _sb_sample_body,
        grid_spec=pltpu.PrefetchScalarGridSpec(
            num_scalar_prefetch=1,
            grid=(n_seq, n_pages // _SB_PAGES),
            in_specs=[pl.BlockSpec((1, T_PAD, W_A), lambda b, j, pt: (b, 0, 0)), seq_page, seq_page,
                      pl.BlockSpec(memory_space=pltpu.SMEM)] + [page_spec(r) for r in range(_SB_PAGES)] * 2,
            out_specs=pl.BlockSpec((1, T_PAD, W_A), lambda b, j, pt: (b, 0, 0)),
            scratch_shapes=[pltpu.VMEM((m, W_A), BF16), pltpu.VMEM((m, 1), F32),
                            pltpu.VMEM((m, W_A), F32), pltpu.VMEM((m, 1), F32)]),
        out_shape=jax.ShapeDtypeStruct((n_seq, T_PAD, W_A), F32),
        compiler_params=_cparams(("parallel", "arbitrary")),
        name="sb_sample",
    )(page_table.reshape(-1), q.reshape(n_seq, T_PAD, W_A), new_page(k_new), new_page(v_new), bias,
      *([as_pages(k_cache)] * _SB_PAGES), *([as_pages(v_cache)] * _SB_PAGES))
    return out.reshape(n_seq * T_PAD, W_A)


def _mem_attn_body(q_ref, k_ref, v_ref, o_ref):
    for h in range(H_C):
        sl = slice(h * HD_C, (h + 1) * HD_C)
        s = _dot_nt(q_ref[:, sl].astype(BF16), k_ref[0, :, sl].astype(BF16)) * (HD_C ** -0.5)
        e = jnp.exp(s - jnp.max(s, axis=-1, keepdims=True))
        prob = e / jnp.sum(e, axis=-1, keepdims=True)
        o_ref[:, sl] = _dot(prob.astype(BF16), v_ref[0, :, sl].astype(BF16))


def _mem_attn(q, mem_k, mem_v, *, n_seq, t_total):
    tq = min(t_total, 512)
    nq = t_total // tq
    mem = pl.BlockSpec((1, N_MEM, W_C), lambda b, i: (b, 0, 0))
    row = pl.BlockSpec((tq, W_C), lambda b, i: (b * nq + i, 0))
    return pl.pallas_call(
        _mem_attn_body,
        grid=(n_seq, nq),
        in_specs=[row, mem, mem],
        out_specs=row,
        out_shape=jax.ShapeDtypeStruct((n_seq * t_total, W_C), F32),
        compiler_params=_cparams(("parallel", "parallel")),
        name="mem_attn",
    )(q, mem_k, mem_v)


def _combine_body(x_ref, oa_ref, ob_ref, oc_ref, gate_ref, wa_ref, wb_ref, wc_ref, wo_ref, g_ref, b_ref, y_ref):
    h = None
    for i, (o_ref, w_ref) in enumerate(((oa_ref, wa_ref), (ob_ref, wb_ref), (oc_ref, wc_ref))):
        part = _sigmoid(gate_ref[:, i * D_MODEL:(i + 1) * D_MODEL]) * _dot(o_ref[...].astype(BF16), w_ref[...])
        h = part if h is None else h + part
    y = DN_ALPHA * x_ref[...] + _dot(h.astype(BF16), wo_ref[...])
    y_ref[...] = _layer_norm(y, g_ref[...], b_ref[...])


def _combine(x, o_a, o_b, o_c, gate, wa, wb, wc, wo, ln_g, ln_b):
    n = x.shape[0]
    bm = min(n, 512)
    row = lambda w: pl.BlockSpec((bm, w), lambda i: (i, 0))
    const = lambda a: pl.BlockSpec(a.shape, lambda i: (0,) * a.ndim, pipeline_mode=pl.Buffered(1))
    return pl.pallas_call(
        _combine_body,
        grid=(n // bm,),
        in_specs=[row(D_MODEL), row(W_A), row(W_A), row(W_C), row(N_GATE)] + [const(a) for a in (wa, wb, wc, wo, ln_g, ln_b)],
        out_specs=row(D_MODEL),
        out_shape=jax.ShapeDtypeStruct((n, D_MODEL), F32),
        compiler_params=_cparams(("parallel",)),
        name="combine",
    )(x, o_a, o_b, o_c, gate, wa, wb, wc, wo, ln_g, ln_b)


def _top2(scores, ids, lowest):
    m1 = jnp.max(scores, axis=0, keepdims=True)
    i1 = jnp.min(jnp.where(scores == m1, ids, N_EXPERTS), axis=0, keepdims=True)
    rest = jnp.where(ids == i1, lowest, scores)
    m2 = jnp.max(rest, axis=0, keepdims=True)
    i2 = jnp.min(jnp.where(rest == m2, ids, N_EXPERTS), axis=0, keepdims=True)
    return m1, i1, m2, i2


def _router_gates(x, wr_t, br):
    m = x.shape[0]
    logits = _dot_f32(wr_t, x, 1, 1) + br
    e = jnp.exp(logits - jnp.max(logits, axis=0, keepdims=True))
    scores = e / jnp.sum(e, axis=0, keepdims=True)
    ids = lax.broadcasted_iota(jnp.int32, (N_EXPERTS, m), 0)
    grp = ids // E_PER_GROUP
    best = None
    for g in range(N_GROUPS):
        m1, _, m2, _ = _top2(jnp.where(grp == g, scores, -1.0), ids, -2.0)
        gs = m1 + m2
        if best is None:
            best, sel = gs, jnp.zeros((1, m), jnp.int32)
        else:
            sel = jnp.where(gs > best, g, sel)
            best = jnp.maximum(gs, best)
    m1, i1, m2, i2 = _top2(jnp.where(grp == sel, scores, -1.0), ids, -2.0)
    tot = m1 + m2
    return jnp.where(ids == i1, m1 / tot, 0.0) + jnp.where(ids == i2, m2 / tot, 0.0)


def _moe_body(x_ref, wr_ref, br_ref, w1_ref, w3_ref, w2_ref, g_ref, b_ref, y_ref, gate_scr, acc):
    e = pl.program_id(1)
    x = x_ref[...]

    @pl.when(e == 0)
    def _():
        gate_scr[...] = _router_gates(x, wr_ref[...], br_ref[...]).T
        acc[...] = jnp.zeros_like(acc)

    lane = lax.broadcasted_iota(jnp.int32, gate_scr.shape, 1)
    ge = jnp.sum(jnp.where(lane == e, gate_scr[...], 0.0), axis=1, keepdims=True)
    xb = x.astype(BF16)
    u = _dot(xb, w1_ref[0])
    h = (u * _sigmoid(u)) * _dot(xb, w3_ref[0])
    acc[...] += _dot((h * ge).astype(BF16), w2_ref[0])

    @pl.when(e == pl.num_programs(1) - 1)
    def _():
        y_ref[...] = _layer_norm(DN_ALPHA * x + acc[...], g_ref[...], b_ref[...])


def _moe(x, wr_t, br, w1, w3, w2, ln_g, ln_b):
    n = x.shape[0]
    bm = min(n, 1024)
    const = lambda a: pl.BlockSpec(a.shape, lambda i, e: (0,) * a.ndim)
    return pl.pallas_call(
        _moe_body,
        grid=(n // bm, N_EXPERTS),
        in_specs=[pl.BlockSpec((bm, D_MODEL), lambda i, e: (i, 0)), const(wr_t), const(br),
                  pl.BlockSpec((1, D_MODEL, D_FF), lambda i, e: (e, 0, 0)),
                  pl.BlockSpec((1, D_MODEL, D_FF), lambda i, e: (e, 0, 0)),
                  pl.BlockSpec((1, D_FF, D_MODEL), lambda i, e: (e, 0, 0)),
                  const(ln_g), const(ln_b)],
        out_specs=pl.BlockSpec((bm, D_MODEL), lambda i, e: (i, 0)),
        out_shape=jax.ShapeDtypeStruct((n, D_MODEL), F32),
        scratch_shapes=[pltpu.VMEM((bm, N_EXPERTS), F32), pltpu.VMEM((bm, D_MODEL), F32)],
        compiler_params=_cparams(("parallel", "arbitrary")),
        name="moe",
    )(x, wr_t, br, w1, w3, w2, ln_g, ln_b)


def _trunk_layer(x, lw, shared, sb_fn, mem_k, mem_v, wkv0, shift0, *, n_seq, t_total, t_valid, kv_channel_major):
    p_shift, q16, k_b, k16, v_b, v16, q_c, gate = _in_proj(x, lw['w_in'], (n_seq, t_total) if kv_channel_major else None)
    prep, shift = _rwkv_prep(p_shift, shift0, lw, n_seq=n_seq, t_total=t_total, t_valid=t_valid)
    o_a, wkv = _wkv(prep, wkv0, lw['gn_g'], lw['gn_b'], n_seq=n_seq, t_total=t_total)
    o_b = sb_fn(q16, k_b, k16, v_b, v16)
    o_c = _mem_attn(q_c, mem_k, mem_v, n_seq=n_seq, t_total=t_total)
    x = _combine(x, o_a, o_b, o_c, gate, lw['w_br_a'], lw['w_br_b'], lw['w_br_c'], lw['w_out'], lw['ln1_g'], lw['ln1_b'])
    x = _moe(x, shared['wr_t'], shared['br'], lw['w1'], lw['w3'], lw['w2'], lw['ln2_g'], lw['ln2_b'])
    return x, k_b, v_b, wkv, shift


def kernel(x_prompt, x_sample, mem_prompt, cache_sb_k, cache_sb_v, cache_mem_k, cache_mem_v, state_wkv, state_shift, page_table, w_in, mu_shift, w0_decay, w_decay2, a0, w_a2, w_g2, k_k, k_a, r_k, gn_g, gn_b, sb_bias, w_mem_kv, w_br_a, w_br_b, w_br_c, w_out, ln1_g, ln1_b, w_router, b_router, w1, w3, w2, ln2_g, ln2_b):
    depth = w_in.shape[0]
    n_p, t_p = x_prompt.shape[:2]
    n_s, t_s = x_sample.shape[:2]
    n_pool = cache_sb_k.shape[1]
    row = lambda a, l: a[l].reshape(1, -1)
    shared = {'wr_t': w_router.T, 'br': b_router.reshape(-1, 1)}

    xp = x_prompt.reshape(n_p * t_p, D_MODEL)
    xs = jnp.pad(x_sample, ((0, 0), (0, T_PAD - t_s), (0, 0))).reshape(n_s * T_PAD, D_MODEL)
    mem_rows = mem_prompt.reshape(n_p * N_MEM, D_MODEL)
    zero_wkv = jnp.zeros((n_p, N_HEAD, HEAD, HEAD), F32)
    zero_shift = jnp.zeros((n_p, N_SHIFT), F32)

    outs = [[] for _ in range(10)]
    for l in range(depth):
        lw = {'w_in': w_in[l].astype(BF16), 'mu_shift': row(mu_shift, l), 'w0_decay': row(w0_decay, l),
              'w_decay2': w_decay2[l].astype(BF16), 'a0': row(a0, l), 'w_a2': w_a2[l].astype(BF16),
              'w_g2': w_g2[l].astype(BF16), 'k_k': row(k_k, l), 'k_a': row(k_a, l), 'r_k': row(r_k, l),
              'gn_g': row(gn_g, l), 'gn_b': row(gn_b, l),
              'w_br_a': w_br_a[l].astype(BF16), 'w_br_b': w_br_b[l].astype(BF16), 'w_br_c': w_br_c[l].astype(BF16),
              'w_out': w_out[l].astype(BF16), 'ln1_g': row(ln1_g, l), 'ln1_b': row(ln1_b, l),
              'w1': w1[l].astype(BF16), 'w3': w3[l].astype(BF16), 'w2': w2[l].astype(BF16),
              'ln2_g': row(ln2_g, l), 'ln2_b': row(ln2_b, l)}
        bias = sb_bias[l]

        mem_kv = _matmul(mem_rows, w_mem_kv[l].astype(BF16))
        mk = mem_kv[:, :W_C].reshape(n_p, N_MEM, W_C)
        mv = mem_kv[:, W_C:].reshape(n_p, N_MEM, W_C)
        sb_p = functools.partial(_sb_prompt, bias=bias, n_seq=n_p, t_total=t_p)
        xp, kb, vb, wkv, sh = _trunk_layer(xp, lw, shared, lambda q16, k, k16, v, v16: sb_p(q16, k16, v16), mk, mv,
                                           zero_wkv, zero_shift, n_seq=n_p, t_total=t_p, t_valid=t_p, kv_channel_major=True)
        for i, a in zip((0, 1, 4, 5, 6, 8), (kb.reshape(n_p, N_HEAD, HEAD, t_p), vb.reshape(n_p, N_HEAD, HEAD, t_p),
                                             mk.reshape(n_p, N_MEM, H_C, HD_C), mv.reshape(n_p, N_MEM, H_C, HD_C), wkv, sh)):
            outs[i].append(a)

        def sb_s(q16, k, k16, v, v16):
            return _sb_sample(q16.astype(F32), k, v, bias, cache_sb_k, cache_sb_v, page_table, l)

        xs, kb, vb, wkv, sh = _trunk_layer(xs, lw, shared, sb_s, cache_mem_k[l].reshape(n_s, N_MEM, W_C),
                                           cache_mem_v[l].reshape(n_s, N_MEM, W_C), state_wkv[l], state_shift[l],
                                           n_seq=n_s, t_total=T_PAD, t_valid=t_s, kv_channel_major=False)
        for i, a in zip((2, 3, 7, 9), (kb.reshape(n_s, T_PAD, N_HEAD, HEAD)[:, :t_s], vb.reshape(n_s, T_PAD, N_HEAD, HEAD)[:, :t_s],
                                       wkv, sh)):
            outs[i].append(a)

    st = [jnp.stack(o) for o in outs]
    for i in (0, 1):
        st[i] = st[i].transpose(0, 1, 4, 2, 3)
    y_p = xp.reshape(n_p, t_p, D_MODEL)
    y_s = xs.reshape(n_s, T_PAD, D_MODEL)[:, :t_s]
    return (y_p, y_s, st[0], st[1], st[2], st[3], st[4], st[5], st[6], st[7], st[8], st[9])
```
